```python
import math
import jax, jax.numpy as jnp
from jax import lax
import numpy as np

D_MODEL = 1024
BATCH = 8
SEQ = 2048
DEPTH = 1

D_MIX = D_MODEL
D_ATTN = D_MIX // 2
D_POOL = D_MIX - D_ATTN
N_DIFF_HEADS = 4
DIFF_QK_DIM = D_ATTN // (2 * N_DIFF_HEADS)
DIFF_V_DIM = 2 * DIFF_QK_DIM
POOL_WINDOWS = (2, 4, 8, 16)
N_POOL_GROUPS = len(POOL_WINDOWS)
POOL_GROUP_DIM = D_POOL // N_POOL_GROUPS
D_IN_PROJ = 3 * D_ATTN + D_POOL
N_EXPERTS = 32
TOP_K = 4
D_FF = D_MODEL
SWIGLU_LIMIT = 7.0
SWIGLU_ALPHA = 1.702
Q_BLOCK = 128
MOE_BLOCK = 128
RMS_EPS = 1e-5

kernel_name = "hymba_style_diffattn_pool_moe_block"


def rmsnorm(x, g):
    xf = x.astype(jnp.float32)
    y = xf * lax.rsqrt(jnp.mean(xf * xf, axis=-1, keepdims=True) + RMS_EPS)
    return (y * g.astype(jnp.float32)).astype(x.dtype)


def diff_attention(q, k, v, lam, subln_g, lambda_init):
    B, S, H, _, DK = q.shape
    nb = S // Q_BLOCK
    qb = jnp.moveaxis(q.reshape(B, nb, Q_BLOCK, H, 2, DK), 1, 0)
    key_pos = jnp.arange(S)
    scale = DK ** -0.5

    def one_block(args):
        i, q_i = args
        q_pos = i * Q_BLOCK + jnp.arange(Q_BLOCK)
        s = jnp.einsum('bqhmd,bkhmd->bhmqk', q_i, k,
                       preferred_element_type=jnp.float32) * scale
        causal = key_pos[None, :] <= q_pos[:, None]
        s = jnp.where(causal, s, -jnp.inf)
        p = jax.nn.softmax(s, axis=-1)
        a = p[:, :, 0] - lam * p[:, :, 1]
        return jnp.einsum('bhqk,bkhd->bqhd', a.astype(v.dtype), v)

    o = lax.map(one_block, (jnp.arange(nb), qb))
    o = jnp.moveaxis(o, 0, 1).reshape(B, S, H, DIFF_V_DIM)
    o = rmsnorm(o, subln_g) * (1.0 - lambda_init)
    return o.reshape(B, S, H * DIFF_V_DIM)


def multiscale_pool(u, w_pool, b_pool, pool_scale):
    B, S, _ = u.shape
    uf = u.astype(jnp.float32).reshape(B, S, N_POOL_GROUPS, POOL_GROUP_DIM)
    csum = jnp.cumsum(uf, axis=1)
    pos1 = jnp.arange(1, S + 1)
    outs = []
    for g, w in enumerate(POOL_WINDOWS):
        c = csum[:, :, g]
        c_prev = jnp.pad(c, ((0, 0), (w, 0), (0, 0)))[:, :S]
        cnt = jnp.minimum(pos1, w).astype(jnp.float32)[None, :, None]
        outs.append((c - c_prev) / cnt - uf[:, :, g])
    z = jnp.stack(outs, axis=2).astype(u.dtype)
    y = jnp.einsum('bsgc,gcd->bsgd', z, w_pool) + b_pool
    return y.reshape(B, S, D_POOL) * pool_scale


def moe(h, router_w, router_b, w_gate, b_gate, w_up, b_up, w_down, b_down):
    T = h.shape[0]
    logits = (h @ router_w + router_b).astype(jnp.float32)
    top_vals, top_idx = lax.top_k(logits, TOP_K)
    gates = jax.nn.softmax(top_vals, axis=-1)
    A = T * TOP_K
    flat_e = top_idx.reshape(A)
    flat_tok = jnp.arange(A, dtype=jnp.int32) // TOP_K
    flat_g = gates.reshape(A)
    order = jnp.argsort(flat_e)
    sorted_e = flat_e[order]
    counts = jnp.bincount(flat_e, length=N_EXPERTS)
    padded = ((counts + MOE_BLOCK - 1) // MOE_BLOCK) * MOE_BLOCK
    group_start = jnp.cumsum(counts) - counts
    padded_end = jnp.cumsum(padded)
    padded_start = padded_end - padded
    dest = padded_start[sorted_e] + (jnp.arange(A) - group_start[sorted_e])
    n_blocks = (A + N_EXPERTS * (MOE_BLOCK - 1) + MOE_BLOCK - 1) // MOE_BLOCK
    P = n_blocks * MOE_BLOCK
    slot_tok = jnp.zeros((P,), jnp.int32).at[dest].set(flat_tok[order])
    slot_gate = jnp.zeros((P,), jnp.float32).at[dest].set(flat_g[order])
    block_expert = jnp.clip(
        jnp.searchsorted(padded_end, jnp.arange(n_blocks) * MOE_BLOCK, side='right'),
        0, N_EXPERTS - 1)

    def expert_block(args):
        e, tok, g = args
        xb = h[tok]
        gate = jnp.minimum(xb @ w_gate[e] + b_gate[e], SWIGLU_LIMIT)
        up = jnp.clip(xb @ w_up[e] + b_up[e], -SWIGLU_LIMIT, SWIGLU_LIMIT)
        act = gate * jax.nn.sigmoid(SWIGLU_ALPHA * gate) * (up + 1.0)
        y = act @ w_down[e] + b_down[e]
        return y * g.astype(y.dtype)[:, None]

    ys = lax.map(expert_block, (block_expert,
                                slot_tok.reshape(n_blocks, MOE_BLOCK),
                                slot_gate.reshape(n_blocks, MOE_BLOCK)))
    return jnp.zeros_like(h).at[slot_tok].add(ys.reshape(P, h.shape[1]))


def setup_inputs(seed: int = 0) -> dict:
    key = jax.random.key(seed)
    ks = jax.random.split(key, 24)
    f32 = jnp.float32
    L = DEPTH

    def nrm(k, shape, scale):
        return jax.random.normal(k, shape, f32) * scale

    return {
        "x": nrm(ks[0], (BATCH, SEQ, D_MODEL), 1.0),
        "norm1_g": 1.0 + nrm(ks[1], (L, D_MODEL), 0.02),
        "w_in": nrm(ks[2], (L, D_MODEL, D_IN_PROJ), D_MODEL ** -0.5),
        "lambda_q1": nrm(ks[3], (L, DIFF_QK_DIM), 0.1),
        "lambda_k1": nrm(ks[4], (L, DIFF_QK_DIM), 0.1),
        "lambda_q2": nrm(ks[5], (L, DIFF_QK_DIM), 0.1),
        "lambda_k2": nrm(ks[6], (L, DIFF_QK_DIM), 0.1),
        "subln_g": 1.0 + nrm(ks[7], (L, DIFF_V_DIM), 0.02),
        "w_pool": nrm(ks[8], (L, N_POOL_GROUPS, POOL_GROUP_DIM, POOL_GROUP_DIM), POOL_GROUP_DIM ** -0.5),
        "b_pool": nrm(ks[9], (L, N_POOL_GROUPS, POOL_GROUP_DIM), 0.01),
        "pool_scale": 1.0 + nrm(ks[10], (L, D_POOL), 0.02),
        "w_out": nrm(ks[11], (L, D_MIX, D_MODEL), D_MIX ** -0.5),
        "norm2_g": 1.0 + nrm(ks[12], (L, D_MODEL), 0.02),
        "router_w": nrm(ks[13], (L, D_MODEL, N_EXPERTS), D_MODEL ** -0.5),
        "router_b": nrm(ks[14], (L, N_EXPERTS), 0.01),
        "w_gate": nrm(ks[15], (L, N_EXPERTS, D_MODEL, D_FF), D_MODEL ** -0.5),
        "b_gate": nrm(ks[16], (L, N_EXPERTS, D_FF), 0.01),
        "w_up": nrm(ks[17], (L, N_EXPERTS, D_MODEL, D_FF), D_MODEL ** -0.5),
        "b_up": nrm(ks[18], (L, N_EXPERTS, D_FF), 0.01),
        "w_down": nrm(ks[19], (L, N_EXPERTS, D_FF, D_MODEL), D_FF ** -0.5),
        "b_down": nrm(ks[20], (L, N_EXPERTS, D_MODEL), 0.01),
        "final_g": 1.0 + nrm(ks[21], (D_MODEL,), 0.02),
    }


def reference(x, norm1_g, w_in, lambda_q1, lambda_k1, lambda_q2, lambda_k2, subln_g,
              w_pool, b_pool, pool_scale, w_out, norm2_g, router_w, router_b,
              w_gate, b_gate, w_up, b_up, w_down, b_down, final_g):
    B, S, D = x.shape
    for l in range(DEPTH):
        lambda_init = 0.8 - 0.6 * math.exp(-0.3 * l)
        lam = (jnp.exp(jnp.sum(lambda_q1[l].astype(jnp.float32) * lambda_k1[l].astype(jnp.float32)))
               - jnp.exp(jnp.sum(lambda_q2[l].astype(jnp.float32) * lambda_k2[l].astype(jnp.float32)))
               + lambda_init)
        h = rmsnorm(x, norm1_g[l])
        proj = h @ w_in[l]
        q, k, v, u = jnp.split(proj, [D_ATTN, 2 * D_ATTN, 3 * D_ATTN], axis=-1)
        q = q.reshape(B, S, N_DIFF_HEADS, 2, DIFF_QK_DIM)
        k = k.reshape(B, S, N_DIFF_HEADS, 2, DIFF_QK_DIM)
        v = v.reshape(B, S, N_DIFF_HEADS, DIFF_V_DIM)
        attn_out = diff_attention(q, k, v, lam, subln_g[l], lambda_init)
        pool_out = multiscale_pool(u, w_pool[l], b_pool[l], pool_scale[l])
        x = x + jnp.concatenate([attn_out, pool_out], axis=-1) @ w_out[l]
        h = rmsnorm(x, norm2_g[l])
        y = moe(h.reshape(B * S, D), router_w[l], router_b[l], w_gate[l], b_gate[l],
                w_up[l], b_up[l], w_down[l], b_down[l])
        x = x + y.reshape(B, S, D)
    return rmsnorm(x, final_g)
```

```python
import functools
import math

import jax
import jax.numpy as jnp
from jax import lax
from jax.experimental import pallas as pl
from jax.experimental.pallas import tpu as pltpu

D_MODEL = 1024
D_ATTN = 512
D_POOL = 512
N_DIFF_HEADS = 4
DIFF_QK_DIM = 64
DIFF_V_DIM = 128
POOL_WINDOWS = (2, 4, 8, 16)
POOL_GROUP_DIM = 128
N_EXPERTS = 32
TOP_K = 4
D_FF = 1024
SWIGLU_LIMIT = 7.0
SWIGLU_ALPHA = 1.702
RMS_EPS = 1e-5

F32 = jnp.float32
BF16 = jnp.bfloat16

TOKEN_TILE = 512
ATTN_Q_TILE = 256
ATTN_KV_TILE = 256
MOE_TILE = 512
GATHER_TILE = 256
FF_CHUNK = 512
VMEM_LIMIT = 56 * 1024 * 1024


def _params(sem, vmem=VMEM_LIMIT):
    return pltpu.CompilerParams(dimension_semantics=sem, vmem_limit_bytes=vmem)


def _inproj_kernel(x_ref, g_ref, w_ref, q_ref, k_ref, v_ref, u_ref):
    x = x_ref[...]
    ms = jnp.mean(x * x, axis=-1, keepdims=True)
    h = (x * lax.rsqrt(ms + RMS_EPS) * g_ref[...]).astype(BF16)
    q = jnp.dot(h, w_ref[:, 0:D_ATTN], preferred_element_type=F32)
    q_ref[...] = (q * (DIFF_QK_DIM ** -0.5)).astype(BF16)
    k_ref[...] = jnp.dot(h, w_ref[:, D_ATTN:2 * D_ATTN], preferred_element_type=F32).astype(BF16)
    v_ref[...] = jnp.dot(h, w_ref[:, 2 * D_ATTN:3 * D_ATTN], preferred_element_type=F32).astype(BF16)
    u_ref[...] = jnp.dot(h, w_ref[:, 3 * D_ATTN:], preferred_element_type=F32)


def _inproj(x2, g, w):
    T = x2.shape[0]
    tm = TOKEN_TILE
    row = lambda i: (i, 0)
    fixed = lambda i: (0, 0)
    return pl.pallas_call(
        _inproj_kernel,
        grid=(T // tm,),
        in_specs=[pl.BlockSpec((tm, D_MODEL), row),
                  pl.BlockSpec((1, D_MODEL), fixed),
                  pl.BlockSpec((D_MODEL, 4 * D_ATTN), fixed)],
        out_specs=[pl.BlockSpec((tm, D_ATTN), row)] * 4,
        out_shape=[jax.ShapeDtypeStruct((T, D_ATTN), BF16)] * 3
        + [jax.ShapeDtypeStruct((T, D_POOL), F32)],
        compiler_params=_params(("arbitrary",)),
        name="inproj",
    )(x2, g, w)


def _attn_kernel(q_ref, k_ref, v_ref, lq1_ref, lk1_ref, lq2_ref, lk2_ref, sg_ref,
                 o_ref, m_ref, l_ref, acc_ref, *, lambda_init):
    tq = q_ref.shape[0]
    tk = ATTN_KV_TILE
    i = pl.program_id(2)

    q = q_ref[...]
    lane = lax.broadcasted_iota(jnp.int32, q.shape, 1)
    zero = jnp.zeros_like(q)
    qs = jnp.concatenate([jnp.where(lane < DIFF_QK_DIM, q, zero),
                          jnp.where(lane >= DIFF_QK_DIM, q, zero)], axis=0)

    m_ref[...] = jnp.full(m_ref.shape, -jnp.inf, F32)
    l_ref[...] = jnp.zeros(l_ref.shape, F32)
    acc_ref[...] = jnp.zeros(acc_ref.shape, F32)

    def step(j, masked):
        kc = k_ref[pl.ds(j * tk, tk), :]
        vc = v_ref[pl.ds(j * tk, tk), :]
        s = lax.dot_general(qs, kc, (((1,), (1,)), ((), ())), preferred_element_type=F32)
        if masked:
            r = lax.broadcasted_iota(jnp.int32, (tq, tk), 0)
            c = lax.broadcasted_iota(jnp.int32, (tq, tk), 1)
            keep = c <= r
            keep = jnp.concatenate([keep, keep], axis=0)
            s = jnp.where(keep, s, -jnp.inf)
        m_old = m_ref[...]
        m_new = jnp.maximum(m_old, jnp.max(s, axis=-1, keepdims=True))
        alpha = jnp.exp(m_old - m_new)
        p = jnp.exp(s - m_new)
        l_ref[...] = alpha * l_ref[...] + jnp.sum(p, axis=-1, keepdims=True)
        acc_ref[...] = alpha * acc_ref[...] + jnp.dot(p.astype(BF16), vc, preferred_element_type=F32)
        m_ref[...] = m_new

    def full_step(j, c):
        step(j, False)
        return c

    lax.fori_loop(0, i, full_step, 0)
    step(i, True)

    lam = (jnp.exp(jnp.sum(lq1_ref[...] * lk1_ref[...]))
           - jnp.exp(jnp.sum(lq2_ref[...] * lk2_ref[...])) + lambda_init)
    o = acc_ref[...] / l_ref[...]
    a = o[:tq] - lam * o[tq:]
    ms = jnp.mean(a * a, axis=-1, keepdims=True)
    y = a * lax.rsqrt(ms + RMS_EPS) * sg_ref[...]
    o_ref[...] = (y * (1.0 - lambda_init)).astype(o_ref.dtype)


def _attention(q, k, v, lq1, lk1, lq2, lk2, subln_g, lambda_init):
    B, S, _ = q.shape
    tq = ATTN_Q_TILE
    assert ATTN_Q_TILE == ATTN_KV_TILE and S % tq == 0
    qmap = lambda b, h, i: (b, i, h)
    kvmap = lambda b, h, i: (b, 0, h)
    vec = lambda n: pl.BlockSpec((1, n), lambda b, h, i: (0, 0))
    return pl.pallas_call(
        functools.partial(_attn_kernel, lambda_init=lambda_init),
        grid=(B, N_DIFF_HEADS, S // tq),
        in_specs=[pl.BlockSpec((None, tq, DIFF_V_DIM), qmap),
                  pl.BlockSpec((None, S, DIFF_V_DIM), kvmap),
                  pl.BlockSpec((None, S, DIFF_V_DIM), kvmap),
                  vec(DIFF_QK_DIM), vec(DIFF_QK_DIM), vec(DIFF_QK_DIM), vec(DIFF_QK_DIM),
                  vec(DIFF_V_DIM)],
        out_specs=pl.BlockSpec((None, tq, DIFF_V_DIM), qmap),
        out_shape=jax.ShapeDtypeStruct((B, S, D_ATTN), BF16),
        scratch_shapes=[pltpu.VMEM((2 * tq, 1), F32),
                        pltpu.VMEM((2 * tq, 1), F32),
                        pltpu.VMEM((2 * tq, DIFF_V_DIM), F32)],
        compiler_params=_params(("arbitrary", "arbitrary", "arbitrary")),
        name="diff_attn",
    )(q, k, v, lq1, lk1, lq2, lk2, subln_g)


def _pool_kernel(u_ref, w_ref, b_ref, sc_ref, o_ref):
    S = u_ref.shape[0]
    row = lax.broadcasted_iota(jnp.int32, (S, POOL_GROUP_DIM), 0)
    for g, w in enumerate(POOL_WINDOWS):
        cols = slice(g * POOL_GROUP_DIM, (g + 1) * POOL_GROUP_DIM)
        ug = u_ref[:, cols]
        s = ug
        span = 1
        while span < w:
            shifted = pltpu.roll(s, shift=span, axis=0)
            s = s + jnp.where(row >= span, shifted, 0.0)
            span *= 2
        cnt = jnp.minimum(row + 1, w).astype(F32)
        z = (s / cnt - ug).astype(BF16)
        y = jnp.dot(z, w_ref[g], preferred_element_type=F32) + b_ref[:, cols]
        o_ref[:, cols] = (y * sc_ref[:, cols]).astype(o_ref.dtype)


def _pool(u, w_pool, b_pool, pool_scale):
    B, S, _ = u.shape
    blk = pl.BlockSpec((None, S, D_POOL), lambda b: (b, 0, 0))
    return pl.pallas_call(
        _pool_kernel,
        grid=(B,),
        in_specs=[blk,
                  pl.BlockSpec((len(POOL_WINDOWS), POOL_GROUP_DIM, POOL_GROUP_DIM), lambda b: (0, 0, 0)),
                  pl.BlockSpec((1, D_POOL), lambda b: (0, 0)),
                  pl.BlockSpec((1, D_POOL), lambda b: (0, 0))],
        out_specs=blk,
        out_shape=jax.ShapeDtypeStruct((B, S, D_POOL), BF16),
        compiler_params=_params(("arbitrary",)),
        name="pool_mixer",
    )(u, w_pool, b_pool, pool_scale)


def _outproj_kernel(a_ref, p_ref, x_ref, w_ref, g_ref, rw_ref, rb_ref, x1_ref, h_ref, lg_ref):
    x1 = (x_ref[...]
          + jnp.dot(a_ref[...], w_ref[0:D_ATTN, :], preferred_element_type=F32)
          + jnp.dot(p_ref[...], w_ref[D_ATTN:, :], preferred_element_type=F32))
    x1_ref[...] = x1
    ms = jnp.mean(x1 * x1, axis=-1, keepdims=True)
    h = x1 * lax.rsqrt(ms + RMS_EPS) * g_ref[...]
    h_ref[...] = h
    lg_ref[...] = jnp.dot(h, rw_ref[...], preferred_element_type=F32,
                          precision=lax.Precision.HIGHEST) + rb_ref[...]


def _outproj(attn, pool, x2, w_out, g2, rw, rb):
    T = x2.shape[0]
    tm = TOKEN_TILE
    row = lambda i: (i, 0)
    fixed = lambda i: (0, 0)
    return pl.pallas_call(
        _outproj_kernel,
        grid=(T // tm,),
        in_specs=[pl.BlockSpec((tm, D_ATTN), row),
                  pl.BlockSpec((tm, D_POOL), row),
                  pl.BlockSpec((tm, D_MODEL), row),
                  pl.BlockSpec((D_MODEL, D_MODEL), fixed),
                  pl.BlockSpec((1, D_MODEL), fixed),
                  pl.BlockSpec((D_MODEL, N_EXPERTS), fixed),
                  pl.BlockSpec((1, N_EXPERTS), fixed)],
        out_specs=[pl.BlockSpec((tm, D_MODEL), row),
                   pl.BlockSpec((tm, D_MODEL), row),
                   pl.BlockSpec((tm, N_EXPERTS), row)],
        out_shape=[jax.ShapeDtypeStruct((T, D_MODEL), F32),
                   jax.ShapeDtypeStruct((T, D_MODEL), F32),
                   jax.ShapeDtypeStruct((T, N_EXPERTS), F32)],
        compiler_params=_params(("arbitrary",)),
        name="outproj_router",
    )(attn, pool, x2, w_out, g2, rw, rb)


def _gather_kernel(idx_ref, src_ref, out_ref, sem):
    tg = out_ref.shape[0]

    def issue(r, c):
        pltpu.make_async_copy(src_ref.at[pl.ds(idx_ref[0, 0, r], 1), :],
                              out_ref.at[pl.ds(r, 1), :], sem).start()
        return c

    lax.fori_loop(0, tg, issue, 0)

    def drain(r, c):
        pltpu.make_async_copy(src_ref.at[pl.ds(0, 1), :], out_ref.at[pl.ds(r, 1), :], sem).wait()
        return c

    lax.fori_loop(0, tg, drain, 0)


def _gather_rows(src, idx):
    M = idx.shape[0]
    tg = GATHER_TILE
    assert M % tg == 0
    D = src.shape[1]
    return pl.pallas_call(
        _gather_kernel,
        grid=(M // tg,),
        in_specs=[pl.BlockSpec((1, 1, tg), lambda i: (i, 0, 0), memory_space=pltpu.SMEM),
                  pl.BlockSpec(memory_space=pl.ANY)],
        out_specs=pl.BlockSpec((tg, D), lambda i: (i, 0)),
        out_shape=jax.ShapeDtypeStruct((M, D), src.dtype),
        scratch_shapes=[pltpu.SemaphoreType.DMA(())],
        compiler_params=_params(("arbitrary",)),
        name="row_gather",
    )(idx.reshape(M // tg, 1, tg), src)


def _moe_kernel(be_ref, nu_ref, xs_ref, sg_ref, wg_ref, bg_ref, wu_ref, bu_ref, wd_ref, bd_ref,
                y_ref, wg_s, wu_s, wd_s):
    i = pl.program_id(0)
    e = be_ref[i]
    prev = be_ref[jnp.maximum(i - 1, 0)]

    @pl.when(jnp.logical_or(i == 0, e != prev))
    def _():
        rows = 128
        for c in range(D_MODEL // rows):
            sl = slice(c * rows, (c + 1) * rows)
            wg_s[sl, :] = wg_ref[sl, :].astype(BF16)
            wu_s[sl, :] = wu_ref[sl, :].astype(BF16)
            wd_s[sl, :] = wd_ref[sl, :].astype(BF16)

    @pl.when(i < nu_ref[0])
    def _():
        x = xs_ref[...].astype(BF16)
        for n in range(D_FF // FF_CHUNK):
            cols = slice(n * FF_CHUNK, (n + 1) * FF_CHUNK)
            g = jnp.dot(x, wg_s[:, cols], preferred_element_type=F32) + bg_ref[:, cols]
            g = jnp.minimum(g, SWIGLU_LIMIT)
            u = jnp.dot(x, wu_s[:, cols], preferred_element_type=F32) + bu_ref[:, cols]
            u = jnp.clip(u, -SWIGLU_LIMIT, SWIGLU_LIMIT)
            act = (g * jax.nn.sigmoid(SWIGLU_ALPHA * g) * (u + 1.0)).astype(BF16)
            part = jnp.dot(act, wd_s[cols, :], preferred_element_type=F32)
            if n == 0:
                y_ref[...] = part
            else:
                y_ref[...] += part
        y_ref[...] = (y_ref[...] + bd_ref[...]) * sg_ref[...]

    @pl.when(i >= nu_ref[0])
    def _():
        y_ref[...] = jnp.zeros(y_ref.shape, y_ref.dtype)


def _moe_experts(block_expert, n_used, xs, slot_gate, wg, bg, wu, bu, wd, bd):
    P = xs.shape[0]
    tm = MOE_TILE
    n_blocks = P // tm
    row = lambda i, be, nu: (i, 0)
    wmap = lambda i, be, nu: (be[i], 0, 0)
    wspec = pl.BlockSpec((None, D_MODEL, D_FF), wmap)
    bspec = pl.BlockSpec((None, 1, D_FF), wmap)
    grid_spec = pltpu.PrefetchScalarGridSpec(
        num_scalar_prefetch=2,
        grid=(n_blocks,),
        in_specs=[pl.BlockSpec((tm, D_MODEL), row),
                  pl.BlockSpec((tm, 1), row),
                  wspec, bspec, wspec, bspec, wspec, bspec],
        out_specs=pl.BlockSpec((tm, D_MODEL), row),
        scratch_shapes=[pltpu.VMEM((D_MODEL, D_FF), BF16),
                        pltpu.VMEM((D_MODEL, D_FF), BF16),
                        pltpu.VMEM((D_FF, D_MODEL), BF16)],
    )
    return pl.pallas_call(
        _moe_kernel,
        grid_spec=grid_spec,
        out_shape=jax.ShapeDtypeStruct((P, D_MODEL), F32),
        compiler_params=_params(("arbitrary",)),
        name="moe_experts",
    )(block_expert, n_used, xs, slot_gate, wg, bg, wu, bu, wd, bd)


def _combine_kernel(x1_ref, y_ref, g_ref, o_ref):
    x = x1_ref[...]
    for k in range(TOP_K):
        x = x + y_ref[k]
    ms = jnp.mean(x * x, axis=-1, keepdims=True)
    o_ref[...] = x * lax.rsqrt(ms + RMS_EPS) * g_ref[...]


def _combine(x1, yk, g):
    T = x1.shape[0]
    tm = TOKEN_TILE
    return pl.pallas_call(
        _combine_kernel,
        grid=(T // tm,),
        in_specs=[pl.BlockSpec((tm, D_MODEL), lambda i: (i, 0)),
                  pl.BlockSpec((TOP_K, tm, D_MODEL), lambda i: (0, i, 0)),
                  pl.BlockSpec((1, D_MODEL), lambda i: (0, 0))],
        out_specs=pl.BlockSpec((tm, D_MODEL), lambda i: (i, 0)),
        out_shape=jax.ShapeDtypeStruct((T, D_MODEL), F32),
        compiler_params=_params(("arbitrary",)),
        name="combine_norm",
    )(x1, yk, g)


def _route(logits):
    T = logits.shape[0]
    A = T * TOP_K
    tm = MOE_TILE
    P = -(-(A + N_EXPERTS * (tm - 1)) // tm) * tm
    n_blocks = P // tm
    top_vals, top_idx = lax.top_k(logits, TOP_K)
    gates = jax.nn.softmax(top_vals, axis=-1)
    flat_e = top_idx.reshape(A).astype(jnp.int32)
    order = jnp.argsort(flat_e).astype(jnp.int32)
    sorted_e = flat_e[order]
    counts = jnp.bincount(flat_e, length=N_EXPERTS).astype(jnp.int32)
    padded = ((counts + tm - 1) // tm) * tm
    group_start = jnp.cumsum(counts) - counts
    padded_end = jnp.cumsum(padded)
    padded_start = padded_end - padded
    dest = padded_start[sorted_e] + (jnp.arange(A, dtype=jnp.int32) - group_start[sorted_e])
    slot_tok = jnp.zeros((P,), jnp.int32).at[dest].set(order // TOP_K)
    slot_gate = jnp.zeros((P,), F32).at[dest].set(gates.reshape(A)[order])
    pos = jnp.zeros((A,), jnp.int32).at[order].set(dest)
    block_expert = jnp.clip(
        jnp.searchsorted(padded_end, jnp.arange(n_blocks, dtype=jnp.int32) * tm, side='right'),
        0, N_EXPERTS - 1).astype(jnp.int32)
    n_used = (padded_end[-1:] // tm).astype(jnp.int32)
    return slot_tok, slot_gate, pos, block_expert, n_used


def kernel(x, norm1_g, w_in, lambda_q1, lambda_k1, lambda_q2, lambda_k2, subln_g, w_pool, b_pool,
           pool_scale, w_out, norm2_g, router_w, router_b, w_gate, b_gate, w_up, b_up, w_down,
           b_down, final_g):
    B, S, D = x.shape
    T = B * S
    l = 0
    lambda_init = 0.8 - 0.6 * math.exp(-0.3 * l)
    x2 = x.reshape(T, D)

    q, k, v, u = _inproj(x2, norm1_g[l][None, :], w_in[l].astype(BF16))
    attn = _attention(q.reshape(B, S, D_ATTN), k.reshape(B, S, D_ATTN), v.reshape(B, S, D_ATTN),
                      lambda_q1[l][None, :], lambda_k1[l][None, :],
                      lambda_q2[l][None, :], lambda_k2[l][None, :],
                      subln_g[l][None, :], lambda_init)
    pool = _pool(u.reshape(B, S, D_POOL), w_pool[l].astype(BF16),
                 b_pool[l].reshape(1, D_POOL), pool_scale[l][None, :])
    x1, h2, logits = _outproj(attn.reshape(T, D_ATTN), pool.reshape(T, D_POOL), x2,
                              w_out[l].astype(BF16), norm2_g[l][None, :],
                              router_w[l], router_b[l][None, :])

    slot_tok, slot_gate, pos, block_expert, n_used = _route(logits)
    xs = _gather_rows(h2, slot_tok)
    ys = _moe_experts(block_expert, n_used, xs, slot_gate[:, None],
                      w_gate[l], b_gate[l][:, None, :], w_up[l], b_up[l][:, None, :],
                      w_down[l], b_down[l][:, None, :])
    yk = _gather_rows(ys, pos.reshape(T, TOP_K).T.reshape(T * TOP_K))
    out = _combine(x1, yk.reshape(TOP_K, T, D), final_g[None, :])
    return out.reshape(B, S, D)
```

```python
import functools
import math

import jax
import jax.numpy as jnp
from jax import lax
from jax.experimental import pallas as pl
from jax.experimental.pallas import tpu as pltpu

D_MODEL = 1024
D_ATTN = 512
D_POOL = 512
N_DIFF_HEADS = 4
DIFF_QK_DIM = 64
DIFF_V_DIM = 128
POOL_WINDOWS = (2, 4, 8, 16)
POOL_GROUP_DIM = 128
N_EXPERTS = 32
TOP_K = 4
D_FF = 1024
SWIGLU_LIMIT = 7.0
SWIGLU_ALPHA = 1.702
RMS_EPS = 1e-5

F32 = jnp.float32
BF16 = jnp.bfloat16

TOKEN_TILE = 512
ATTN_TILE = 256
MOE_TILE = 512
GATHER_TILE = 256
FF_CHUNK = 512
VMEM_LIMIT = 56 * 1024 * 1024


def _params(sem, vmem=VMEM_LIMIT):
    return pltpu.CompilerParams(dimension_semantics=sem, vmem_limit_bytes=vmem)


def _inproj_kernel(x_ref, g_ref, wqv_ref, wk_ref, wu_ref, qt_ref, k_ref, vt_ref, u_ref):
    x = x_ref[...]
    ms = jnp.mean(x * x, axis=-1, keepdims=True)
    h = (x * lax.rsqrt(ms + RMS_EPS) * g_ref[...]).astype(BF16)
    qvt = lax.dot_general(wqv_ref[...], h, (((1,), (1,)), ((), ())), preferred_element_type=F32)
    qt_ref[...] = (qvt[:D_ATTN] * (DIFF_QK_DIM ** -0.5)).astype(BF16)
    vt_ref[...] = qvt[D_ATTN:].astype(BF16)
    k_ref[...] = jnp.dot(h, wk_ref[...], preferred_element_type=F32).astype(BF16)
    u_ref[...] = jnp.dot(h, wu_ref[...], preferred_element_type=F32)


def _inproj(x2, g, wqv_t, wk, wu, B, S):
    T = x2.shape[0]
    tm = TOKEN_TILE
    per_seq = S // tm
    row = lambda i: (i, 0)
    fixed = lambda i: (0, 0)
    tmap = lambda i: (i // per_seq, 0, i % per_seq)
    return pl.pallas_call(
        _inproj_kernel,
        grid=(T // tm,),
        in_specs=[pl.BlockSpec((tm, D_MODEL), row),
                  pl.BlockSpec((1, D_MODEL), fixed),
                  pl.BlockSpec((2 * D_ATTN, D_MODEL), fixed),
                  pl.BlockSpec((D_MODEL, D_ATTN), fixed),
                  pl.BlockSpec((D_MODEL, D_POOL), fixed)],
        out_specs=[pl.BlockSpec((None, D_ATTN, tm), tmap),
                   pl.BlockSpec((tm, D_ATTN), row),
                   pl.BlockSpec((None, D_ATTN, tm), tmap),
                   pl.BlockSpec((tm, D_POOL), row)],
        out_shape=[jax.ShapeDtypeStruct((B, D_ATTN, S), BF16),
                   jax.ShapeDtypeStruct((T, D_ATTN), BF16),
                   jax.ShapeDtypeStruct((B, D_ATTN, S), BF16),
                   jax.ShapeDtypeStruct((T, D_POOL), F32)],
        compiler_params=_params(("arbitrary",)),
        name="inproj",
    )(x2, g, wqv_t, wk, wu)


def _attn_kernel(qa_ref, qb_ref, k_ref, vt_ref, lq1_ref, lk1_ref, lq2_ref, lk2_ref, sg_ref,
                 o_ref, qs_s, vt_s, m_s, acc_s, *, lambda_init, n_q):
    tq = qa_ref.shape[1]
    tk = tq
    dv = DIFF_V_DIM
    p = pl.program_id(2)

    @pl.when(p == 0)
    def _():
        for c in range(n_q):
            vt_s[c, 0:dv, :] = vt_ref[:, c * tk:(c + 1) * tk]
            vt_s[c, dv:, :] = jnp.ones((vt_s.shape[1] - dv, tk), BF16)

    feat = lax.broadcasted_iota(jnp.int32, (dv, tq), 0)
    for blk, q_ref in enumerate((qa_ref, qb_ref)):
        qt = q_ref[...]
        zero = jnp.zeros_like(qt)
        qs_s[blk] = jnp.concatenate([jnp.where(feat < DIFF_QK_DIM, qt, zero),
                                     jnp.where(feat >= DIFF_QK_DIM, qt, zero)], axis=1)
    m_s[...] = jnp.full(m_s.shape, -jnp.inf, F32)
    acc_s[...] = jnp.zeros(acc_s.shape, F32)

    key = lax.broadcasted_iota(jnp.int32, (tk, tq), 0)
    qry = lax.broadcasted_iota(jnp.int32, (tk, tq), 1)
    bias = jnp.where(key <= qry, 0.0, -jnp.inf).astype(F32)
    bias = jnp.concatenate([bias, bias], axis=1)

    items = [(0, p, True), (1, n_q - 1 - p, True)]
    for n in range(n_q - 1):
        in_a = n < p
        items.append((jnp.where(in_a, 0, 1), jnp.where(in_a, n, n - p), False))

    def scores(blk, chunk, diag):
        kc = k_ref[pl.ds(pl.multiple_of(chunk * tk, tk), tk), :]
        s = jnp.dot(kc, qs_s[blk], preferred_element_type=F32)
        return s + bias if diag else s

    for blk, chunk, diag in items:
        s = scores(blk, chunk, diag)
        m_s[blk] = jnp.maximum(m_s[blk], jnp.max(s, axis=0, keepdims=True))

    for blk, chunk, diag in items:
        s = scores(blk, chunk, diag)
        pt = jnp.exp(s - m_s[blk]).astype(BF16)
        acc_s[blk] += jnp.dot(vt_s[chunk], pt, preferred_element_type=F32)

    lam = (jnp.exp(jnp.sum(lq1_ref[...] * lk1_ref[...]))
           - jnp.exp(jnp.sum(lq2_ref[...] * lk2_ref[...])) + lambda_init)
    for blk, qblock in enumerate((p, n_q - 1 - p)):
        acc = acc_s[blk]
        o0 = acc[0:dv, 0:tq] / acc[dv:dv + 1, 0:tq]
        o1 = acc[0:dv, tq:] / acc[dv:dv + 1, tq:]
        a = o0 - lam * o1
        ms = jnp.mean(a * a, axis=0, keepdims=True)
        y = (a * lax.rsqrt(ms + RMS_EPS)).T * sg_ref[...]
        o_ref[pl.ds(pl.multiple_of(qblock * tq, tq), tq), :] = (y * (1.0 - lambda_init)).astype(o_ref.dtype)


def _attention(qt, k, vt, lq1, lk1, lq2, lk2, subln_g, lambda_init):
    B, _, S = qt.shape
    tq = ATTN_TILE
    n_q = S // tq
    assert S % tq == 0 and n_q % 2 == 0
    ones_rows = 16
    vec = lambda n: pl.BlockSpec((1, n), lambda b, h, p: (0, 0))
    return pl.pallas_call(
        functools.partial(_attn_kernel, lambda_init=lambda_init, n_q=n_q),
        grid=(B, N_DIFF_HEADS, n_q // 2),
        in_specs=[pl.BlockSpec((None, DIFF_V_DIM, tq), lambda b, h, p: (b, h, p)),
                  pl.BlockSpec((None, DIFF_V_DIM, tq), lambda b, h, p: (b, h, n_q - 1 - p)),
                  pl.BlockSpec((None, S, DIFF_V_DIM), lambda b, h, p: (b, 0, h)),
                  pl.BlockSpec((None, DIFF_V_DIM, S), lambda b, h, p: (b, h, 0)),
                  vec(DIFF_QK_DIM), vec(DIFF_QK_DIM), vec(DIFF_QK_DIM), vec(DIFF_QK_DIM),
                  vec(DIFF_V_DIM)],
        out_specs=pl.BlockSpec((None, S, DIFF_V_DIM), lambda b, h, p: (b, 0, h)),
        out_shape=jax.ShapeDtypeStruct((B, S, D_ATTN), BF16),
        scratch_shapes=[pltpu.VMEM((2, DIFF_V_DIM, 2 * tq), BF16),
                        pltpu.VMEM((n_q, DIFF_V_DIM + ones_rows, tq), BF16),
                        pltpu.VMEM((2, 1, 2 * tq), F32),
                        pltpu.VMEM((2, DIFF_V_DIM + ones_rows, 2 * tq), F32)],
        compiler_params=_params(("arbitrary", "arbitrary", "arbitrary")),
        name="diff_attn",
    )(qt, qt, k, vt, lq1, lk1, lq2, lk2, subln_g)


def _pool_kernel(u_ref, w_ref, b_ref, sc_ref, o_ref):
    S = u_ref.shape[0]
    row = lax.broadcasted_iota(jnp.int32, (S, POOL_GROUP_DIM), 0)
    for g, w in enumerate(POOL_WINDOWS):
        cols = slice(g * POOL_GROUP_DIM, (g + 1) * POOL_GROUP_DIM)
        ug = u_ref[:, cols]
        s = ug
        span = 1
        while span < w:
            shifted = pltpu.roll(s, shift=span, axis=0)
            s = s + jnp.where(row >= span, shifted, 0.0)
            span *= 2
        cnt = jnp.minimum(row + 1, w).astype(F32)
        z = (s / cnt - ug).astype(BF16)
        y = jnp.dot(z, w_ref[g], preferred_element_type=F32) + b_ref[:, cols]
        o_ref[:, cols] = (y * sc_ref[:, cols]).astype(o_ref.dtype)


def _pool(u, w_pool, b_pool, pool_scale):
    B, S, _ = u.shape
    blk = pl.BlockSpec((None, S, D_POOL), lambda b: (b, 0, 0))
    return pl.pallas_call(
        _pool_kernel,
        grid=(B,),
        in_specs=[blk,
                  pl.BlockSpec((len(POOL_WINDOWS), POOL_GROUP_DIM, POOL_GROUP_DIM), lambda b: (0, 0, 0)),
                  pl.BlockSpec((1, D_POOL), lambda b: (0, 0)),
                  pl.BlockSpec((1, D_POOL), lambda b: (0, 0))],
        out_specs=blk,
        out_shape=jax.ShapeDtypeStruct((B, S, D_POOL), BF16),
        compiler_params=_params(("arbitrary",)),
        name="pool_mixer",
    )(u, w_pool, b_pool, pool_scale)


def _outproj_kernel(a_ref, p_ref, x_ref, w_ref, g_ref, rw_ref, rb_ref, x1_ref, h_ref, lg_ref):
    x1 = (x_ref[...]
          + jnp.dot(a_ref[...], w_ref[0:D_ATTN, :], preferred_element_type=F32)
          + jnp.dot(p_ref[...], w_ref[D_ATTN:, :], preferred_element_type=F32))
    x1_ref[...] = x1
    ms = jnp.mean(x1 * x1, axis=-1, keepdims=True)
    h = x1 * lax.rsqrt(ms + RMS_EPS) * g_ref[...]
    h_ref[...] = h
    lg_ref[...] = jnp.dot(h, rw_ref[...], preferred_element_type=F32,
                          precision=lax.Precision.HIGHEST) + rb_ref[...]


def _outproj(attn, pool, x2, w_out, g2, rw, rb):
    T = x2.shape[0]
    tm = TOKEN_TILE
    row = lambda i: (i, 0)
    fixed = lambda i: (0, 0)
    return pl.pallas_call(
        _outproj_kernel,
        grid=(T // tm,),
        in_specs=[pl.BlockSpec((tm, D_ATTN), row),
                  pl.BlockSpec((tm, D_POOL), row),
                  pl.BlockSpec((tm, D_MODEL), row),
                  pl.BlockSpec((D_MODEL, D_MODEL), fixed),
                  pl.BlockSpec((1, D_MODEL), fixed),
                  pl.BlockSpec((D_MODEL, N_EXPERTS), fixed),
                  pl.BlockSpec((1, N_EXPERTS), fixed)],
        out_specs=[pl.BlockSpec((tm, D_MODEL), row),
                   pl.BlockSpec((tm, D_MODEL), row),
                   pl.BlockSpec((tm, N_EXPERTS), row)],
        out_shape=[jax.ShapeDtypeStruct((T, D_MODEL), F32),
                   jax.ShapeDtypeStruct((T, D_MODEL), F32),
                   jax.ShapeDtypeStruct((T, N_EXPERTS), F32)],
        compiler_params=_params(("arbitrary",)),
        name="outproj_router",
    )(attn, pool, x2, w_out, g2, rw, rb)


def _gather_kernel(idx_ref, src_ref, out_ref, sem):
    tg = out_ref.shape[0]

    def issue(r, c):
        pltpu.make_async_copy(src_ref.at[pl.ds(idx_ref[0, 0, r], 1), :],
                              out_ref.at[pl.ds(r, 1), :], sem).start()
        return c

    lax.fori_loop(0, tg, issue, 0)

    def drain(r, c):
        pltpu.make_async_copy(src_ref.at[pl.ds(0, 1), :], out_ref.at[pl.ds(r, 1), :], sem).wait()
        return c

    lax.fori_loop(0, tg, drain, 0)


def _gather_rows(src, idx):
    M = idx.shape[0]
    tg = GATHER_TILE
    assert M % tg == 0
    D = src.shape[1]
    return pl.pallas_call(
        _gather_kernel,
        grid=(M // tg,),
        in_specs=[pl.BlockSpec((1, 1, tg), lambda i: (i, 0, 0), memory_space=pltpu.SMEM),
                  pl.BlockSpec(memory_space=pl.ANY)],
        out_specs=pl.BlockSpec((tg, D), lambda i: (i, 0)),
        out_shape=jax.ShapeDtypeStruct((M, D), src.dtype),
        scratch_shapes=[pltpu.SemaphoreType.DMA(())],
        compiler_params=_params(("arbitrary",)),
        name="row_gather",
    )(idx.reshape(M // tg, 1, tg), src)


def _moe_kernel(be_ref, nu_ref, xs_ref, sg_ref, wg_ref, bg_ref, wu_ref, bu_ref, wd_ref, bd_ref,
                y_ref, wg_s, wu_s, wd_s):
    i = pl.program_id(0)
    e = be_ref[i]
    prev = be_ref[jnp.maximum(i - 1, 0)]

    @pl.when(jnp.logical_or(i == 0, e != prev))
    def _():
        rows = 128
        for c in range(D_MODEL // rows):
            sl = slice(c * rows, (c + 1) * rows)
            wg_s[sl, :] = wg_ref[sl, :].astype(BF16)
            wu_s[sl, :] = wu_ref[sl, :].astype(BF16)
            wd_s[sl, :] = wd_ref[sl, :].astype(BF16)

    @pl.when(i < nu_ref[0])
    def _():
        x = xs_ref[...].astype(BF16)
        for n in range(D_FF // FF_CHUNK):
            cols = slice(n * FF_CHUNK, (n + 1) * FF_CHUNK)
            g = jnp.dot(x, wg_s[:, cols], preferred_element_type=F32) + bg_ref[:, cols]
            g = jnp.minimum(g, SWIGLU_LIMIT)
            u = jnp.dot(x, wu_s[:, cols], preferred_element_type=F32) + bu_ref[:, cols]
            u = jnp.clip(u, -SWIGLU_LIMIT, SWIGLU_LIMIT)
            act = (g * jax.nn.sigmoid(SWIGLU_ALPHA * g) * (u + 1.0)).astype(BF16)
            part = jnp.dot(act, wd_s[cols, :], preferred_element_type=F32)
            if n == 0:
                y_ref[...] = part
            else:
                y_ref[...] += part
        y_ref[...] = (y_ref[...] + bd_ref[...]) * sg_ref[...]

    @pl.when(i >= nu_ref[0])
    def _():
        y_ref[...] = jnp.zeros(y_ref.shape, y_ref.dtype)


def _moe_experts(block_expert, n_used, xs, slot_gate, wg, bg, wu, bu, wd, bd):
    P = xs.shape[0]
    tm = MOE_TILE
    n_blocks = P // tm
    row = lambda i, be, nu: (i, 0)
    wmap = lambda i, be, nu: (be[i], 0, 0)
    wspec = pl.BlockSpec((None, D_MODEL, D_FF), wmap)
    bspec = pl.BlockSpec((None, 1, D_FF), wmap)
    grid_spec = pltpu.PrefetchScalarGridSpec(
        num_scalar_prefetch=2,
        grid=(n_blocks,),
        in_specs=[pl.BlockSpec((tm, D_MODEL), row),
                  pl.BlockSpec((tm, 1), row),
                  wspec, bspec, wspec, bspec, wspec, bspec],
        out_specs=pl.BlockSpec((tm, D_MODEL), row),
        scratch_shapes=[pltpu.VMEM((D_MODEL, D_FF), BF16),
                        pltpu.VMEM((D_MODEL, D_FF), BF16),
                        pltpu.VMEM((D_FF, D_MODEL), BF16)],
    )
    return pl.pallas_call(
        _moe_kernel,
        grid_spec=grid_spec,
        out_shape=jax.ShapeDtypeStruct((P, D_MODEL), F32),
        compiler_params=_params(("arbitrary",)),
        name="moe_experts",
    )(block_expert, n_used, xs, slot_gate, wg, bg, wu, bu, wd, bd)


def _combine_kernel(x1_ref, y_ref, g_ref, o_ref):
    x = x1_ref[...]
    for k in range(TOP_K):
        x = x + y_ref[k]
    ms = jnp.mean(x * x, axis=-1, keepdims=True)
    o_ref[...] = x * lax.rsqrt(ms + RMS_EPS) * g_ref[...]


def _combine(x1, yk, g):
    T = x1.shape[0]
    tm = TOKEN_TILE
    return pl.pallas_call(
        _combine_kernel,
        grid=(T // tm,),
        in_specs=[pl.BlockSpec((tm, D_MODEL), lambda i: (i, 0)),
                  pl.BlockSpec((TOP_K, tm, D_MODEL), lambda i: (0, i, 0)),
                  pl.BlockSpec((1, D_MODEL), lambda i: (0, 0))],
        out_specs=pl.BlockSpec((tm, D_MODEL), lambda i: (i, 0)),
        out_shape=jax.ShapeDtypeStruct((T, D_MODEL), F32),
        compiler_params=_params(("arbitrary",)),
        name="combine_norm",
    )(x1, yk, g)


def _route(logits):
    T = logits.shape[0]
    A = T * TOP_K
    tm = MOE_TILE
    P = -(-(A + N_EXPERTS * (tm - 1)) // tm) * tm
    n_blocks = P // tm
    top_vals, top_idx = lax.top_k(logits, TOP_K)
    gates = jax.nn.softmax(top_vals, axis=-1)
    flat_e = top_idx.reshape(A).astype(jnp.int32)
    order = jnp.argsort(flat_e).astype(jnp.int32)
    sorted_e = flat_e[order]
    counts = jnp.bincount(flat_e, length=N_EXPERTS).astype(jnp.int32)
    padded = ((counts + tm - 1) // tm) * tm
    group_start = jnp.cumsum(counts) - counts
    padded_end = jnp.cumsum(padded)
    padded_start = padded_end - padded
    dest = padded_start[sorted_e] + (jnp.arange(A, dtype=jnp.int32) - group_start[sorted_e])
    slot_tok = jnp.zeros((P,), jnp.int32).at[dest].set(order // TOP_K)
    slot_gate = jnp.zeros((P,), F32).at[dest].set(gates.reshape(A)[order])
    pos = jnp.zeros((A,), jnp.int32).at[order].set(dest)
    block_expert = jnp.clip(
        jnp.searchsorted(padded_end, jnp.arange(n_blocks, dtype=jnp.int32) * tm, side='right'),
        0, N_EXPERTS - 1).astype(jnp.int32)
    n_used = (padded_end[-1:] // tm).astype(jnp.int32)
    return slot_tok, slot_gate, pos, block_expert, n_used


def kernel(x, norm1_g, w_in, lambda_q1, lambda_k1, lambda_q2, lambda_k2, subln_g, w_pool, b_pool,
           pool_scale, w_out, norm2_g, router_w, router_b, w_gate, b_gate, w_up, b_up, w_down,
           b_down, final_g):
    B, S, D = x.shape
    T = B * S
    l = 0
    lambda_init = 0.8 - 0.6 * math.exp(-0.3 * l)
    x2 = x.reshape(T, D)

    w = w_in[l]
    wqv_t = jnp.concatenate([w[:, :D_ATTN], w[:, 2 * D_ATTN:3 * D_ATTN]], axis=1).T.astype(BF16)
    qt, k, vt, u = _inproj(x2, norm1_g[l][None, :], wqv_t, w[:, D_ATTN:2 * D_ATTN].astype(BF16),
                           w[:, 3 * D_ATTN:].astype(BF16), B, S)
    attn = _attention(qt, k.reshape(B, S, D_ATTN), vt,
                      lambda_q1[l][None, :], lambda_k1[l][None, :],
                      lambda_q2[l][None, :], lambda_k2[l][None, :],
                      subln_g[l][None, :], lambda_init)
    pool = _pool(u.reshape(B, S, D_POOL), w_pool[l].astype(BF16),
                 b_pool[l].reshape(1, D_POOL), pool_scale[l][None, :])
    x1, h2, logits = _outproj(attn.reshape(T, D_ATTN), pool.reshape(T, D_POOL), x2,
                              w_out[l].astype(BF16), norm2_g[l][None, :],
                              router_w[l], router_b[l][None, :])

    slot_tok, slot_gate, pos, block_expert, n_used = _route(logits)
    xs = _gather_rows(h2, slot_tok)
    ys = _moe_experts(block_expert, n_used, xs, slot_gate[:, None],
                      w_gate[l], b_gate[l][:, None, :], w_up[l], b_up[l][:, None, :],
                      w_down[l], b_down[l][:, None, :])
    yk = _gather_rows(ys, pos.reshape(T, TOP_K).T.reshape(T * TOP_K))
    out = _combine(x1, yk.reshape(TOP_K, T, D), final_g[None, :])
    return out.reshape(B, S, D)
```

```python
import functools
import math

import jax
import jax.numpy as jnp
from jax import lax
from jax.experimental import pallas as pl
from jax.experimental.pallas import tpu as pltpu

D_MODEL = 1024
D_ATTN = 512
D_POOL = 512
N_DIFF_HEADS = 4
DIFF_QK_DIM = 64
DIFF_V_DIM = 128
POOL_WINDOWS = (2, 4, 8, 16)
POOL_GROUP_DIM = 128
N_EXPERTS = 32
TOP_K = 4
D_FF = 1024
SWIGLU_LIMIT = 7.0
SWIGLU_ALPHA = 1.702
RMS_EPS = 1e-5

F32 = jnp.float32
BF16 = jnp.bfloat16

TOKEN_TILE = 512
ATTN_TILE = 256
MOE_TILE = 512
GATHER_TILE = 256
FF_CHUNK = 512
VMEM_LIMIT = 56 * 1024 * 1024


def _params(sem, vmem=VMEM_LIMIT):
    return pltpu.CompilerParams(dimension_semantics=sem, vmem_limit_bytes=vmem)


LANES = 128
TILE_ROWS = D_MODEL // LANES
assert TILE_ROWS == 8


def _store_token_tiles(ref, x):
    n = x.shape[0]
    for c in range(TILE_ROWS):
        ref[pl.ds(c, n, stride=TILE_ROWS), :] = x[:, c * LANES:(c + 1) * LANES]


def _load_token_tiles(ref, n, dtype=None):
    cols = [ref[pl.ds(c, n, stride=TILE_ROWS), :] for c in range(TILE_ROWS)]
    if dtype is not None:
        cols = [c.astype(dtype) for c in cols]
    return jnp.concatenate(cols, axis=1)


def _inproj_kernel(x_ref, g_ref, wqv_ref, wk_ref, wu_ref, qt_ref, k_ref, vt_ref, u_ref):
    x = x_ref[...]
    ms = jnp.mean(x * x, axis=-1, keepdims=True)
    h = (x * lax.rsqrt(ms + RMS_EPS) * g_ref[...]).astype(BF16)
    qvt = lax.dot_general(wqv_ref[...], h, (((1,), (1,)), ((), ())), preferred_element_type=F32)
    qt_ref[...] = (qvt[:D_ATTN] * (DIFF_QK_DIM ** -0.5)).astype(BF16)
    vt_ref[...] = qvt[D_ATTN:].astype(BF16)
    k_ref[...] = jnp.dot(h, wk_ref[...], preferred_element_type=F32).astype(BF16)
    u_ref[...] = jnp.dot(h, wu_ref[...], preferred_element_type=F32)


def _inproj(x2, g, wqv_t, wk, wu, B, S):
    T = x2.shape[0]
    tm = TOKEN_TILE
    per_seq = S // tm
    row = lambda i: (i, 0)
    fixed = lambda i: (0, 0)
    tmap = lambda i: (i // per_seq, 0, i % per_seq)
    return pl.pallas_call(
        _inproj_kernel,
        grid=(T // tm,),
        in_specs=[pl.BlockSpec((tm, D_MODEL), row),
                  pl.BlockSpec((1, D_MODEL), fixed),
                  pl.BlockSpec((2 * D_ATTN, D_MODEL), fixed),
                  pl.BlockSpec((D_MODEL, D_ATTN), fixed),
                  pl.BlockSpec((D_MODEL, D_POOL), fixed)],
        out_specs=[pl.BlockSpec((None, D_ATTN, tm), tmap),
                   pl.BlockSpec((tm, D_ATTN), row),
                   pl.BlockSpec((None, D_ATTN, tm), tmap),
                   pl.BlockSpec((tm, D_POOL), row)],
        out_shape=[jax.ShapeDtypeStruct((B, D_ATTN, S), BF16),
                   jax.ShapeDtypeStruct((T, D_ATTN), BF16),
                   jax.ShapeDtypeStruct((B, D_ATTN, S), BF16),
                   jax.ShapeDtypeStruct((T, D_POOL), F32)],
        compiler_params=_params(("arbitrary",)),
        name="inproj",
    )(x2, g, wqv_t, wk, wu)


def _attn_kernel(qa_ref, qb_ref, k_ref, vt_ref, lq1_ref, lk1_ref, lq2_ref, lk2_ref, sg_ref,
                 o_ref, qs_s, vt_s, m_s, acc_s, *, lambda_init, n_q):
    tq = qa_ref.shape[1]
    tk = tq
    dv = DIFF_V_DIM
    p = pl.program_id(2)

    @pl.when(p == 0)
    def _():
        for c in range(n_q):
            vt_s[c, 0:dv, :] = vt_ref[:, c * tk:(c + 1) * tk]
            vt_s[c, dv:, :] = jnp.ones((vt_s.shape[1] - dv, tk), BF16)

    feat = lax.broadcasted_iota(jnp.int32, (dv, tq), 0)
    for blk, q_ref in enumerate((qa_ref, qb_ref)):
        qt = q_ref[...]
        zero = jnp.zeros_like(qt)
        qs_s[blk] = jnp.concatenate([jnp.where(feat < DIFF_QK_DIM, qt, zero),
                                     jnp.where(feat >= DIFF_QK_DIM, qt, zero)], axis=1)
    m_s[...] = jnp.full(m_s.shape, -jnp.inf, F32)
    acc_s[...] = jnp.zeros(acc_s.shape, F32)

    key = lax.broadcasted_iota(jnp.int32, (tk, tq), 0)
    qry = lax.broadcasted_iota(jnp.int32, (tk, tq), 1)
    bias = jnp.where(key <= qry, 0.0, -jnp.inf).astype(F32)
    bias = jnp.concatenate([bias, bias], axis=1)

    items = [(0, p, True), (1, n_q - 1 - p, True)]
    for n in range(n_q - 1):
        in_a = n < p
        items.append((jnp.where(in_a, 0, 1), jnp.where(in_a, n, n - p), False))

    def scores(blk, chunk, diag):
        kc = k_ref[pl.ds(pl.multiple_of(chunk * tk, tk), tk), :]
        s = jnp.dot(kc, qs_s[blk], preferred_element_type=F32)
        return s + bias if diag else s

    for blk, chunk, diag in items:
        s = scores(blk, chunk, diag)
        m_s[blk] = jnp.maximum(m_s[blk], jnp.max(s, axis=0, keepdims=True))

    for blk, chunk, diag in items:
        s = scores(blk, chunk, diag)
        pt = jnp.exp(s - m_s[blk]).astype(BF16)
        acc_s[blk] += jnp.dot(vt_s[chunk], pt, preferred_element_type=F32)

    lam = (jnp.exp(jnp.sum(lq1_ref[...] * lk1_ref[...]))
           - jnp.exp(jnp.sum(lq2_ref[...] * lk2_ref[...])) + lambda_init)
    for blk, qblock in enumerate((p, n_q - 1 - p)):
        acc = acc_s[blk]
        o0 = acc[0:dv, 0:tq] / acc[dv:dv + 1, 0:tq]
        o1 = acc[0:dv, tq:] / acc[dv:dv + 1, tq:]
        a = o0 - lam * o1
        ms = jnp.mean(a * a, axis=0, keepdims=True)
        y = (a * lax.rsqrt(ms + RMS_EPS)).T * sg_ref[...]
        o_ref[pl.ds(pl.multiple_of(qblock * tq, tq), tq), :] = (y * (1.0 - lambda_init)).astype(o_ref.dtype)


def _attention(qt, k, vt, lq1, lk1, lq2, lk2, subln_g, lambda_init):
    B, _, S = qt.shape
    tq = ATTN_TILE
    n_q = S // tq
    assert S % tq == 0 and n_q % 2 == 0
    ones_rows = 16
    vec = lambda n: pl.BlockSpec((1, n), lambda b, h, p: (0, 0))
    return pl.pallas_call(
        functools.partial(_attn_kernel, lambda_init=lambda_init, n_q=n_q),
        grid=(B, N_DIFF_HEADS, n_q // 2),
        in_specs=[pl.BlockSpec((None, DIFF_V_DIM, tq), lambda b, h, p: (b, h, p)),
                  pl.BlockSpec((None, DIFF_V_DIM, tq), lambda b, h, p: (b, h, n_q - 1 - p)),
                  pl.BlockSpec((None, S, DIFF_V_DIM), lambda b, h, p: (b, 0, h)),
                  pl.BlockSpec((None, DIFF_V_DIM, S), lambda b, h, p: (b, h, 0)),
                  vec(DIFF_QK_DIM), vec(DIFF_QK_DIM), vec(DIFF_QK_DIM), vec(DIFF_QK_DIM),
                  vec(DIFF_V_DIM)],
        out_specs=pl.BlockSpec((None, S, DIFF_V_DIM), lambda b, h, p: (b, 0, h)),
        out_shape=jax.ShapeDtypeStruct((B, S, D_ATTN), BF16),
        scratch_shapes=[pltpu.VMEM((2, DIFF_V_DIM, 2 * tq), BF16),
                        pltpu.VMEM((n_q, DIFF_V_DIM + ones_rows, tq), BF16),
                        pltpu.VMEM((2, 1, 2 * tq), F32),
                        pltpu.VMEM((2, DIFF_V_DIM + ones_rows, 2 * tq), F32)],
        compiler_params=_params(("arbitrary", "arbitrary", "arbitrary")),
        name="diff_attn",
    )(qt, qt, k, vt, lq1, lk1, lq2, lk2, subln_g)


def _pool_kernel(u_ref, w_ref, b_ref, sc_ref, o_ref):
    S = u_ref.shape[0]
    row = lax.broadcasted_iota(jnp.int32, (S, POOL_GROUP_DIM), 0)
    for g, w in enumerate(POOL_WINDOWS):
        cols = slice(g * POOL_GROUP_DIM, (g + 1) * POOL_GROUP_DIM)
        ug = u_ref[:, cols]
        s = ug
        span = 1
        while span < w:
            shifted = pltpu.roll(s, shift=span, axis=0)
            s = s + jnp.where(row >= span, shifted, 0.0)
            span *= 2
        cnt = jnp.minimum(row + 1, w).astype(F32)
        z = (s / cnt - ug).astype(BF16)
        y = jnp.dot(z, w_ref[g], preferred_element_type=F32) + b_ref[:, cols]
        o_ref[:, cols] = (y * sc_ref[:, cols]).astype(o_ref.dtype)


def _pool(u, w_pool, b_pool, pool_scale):
    B, S, _ = u.shape
    blk = pl.BlockSpec((None, S, D_POOL), lambda b: (b, 0, 0))
    return pl.pallas_call(
        _pool_kernel,
        grid=(B,),
        in_specs=[blk,
                  pl.BlockSpec((len(POOL_WINDOWS), POOL_GROUP_DIM, POOL_GROUP_DIM), lambda b: (0, 0, 0)),
                  pl.BlockSpec((1, D_POOL), lambda b: (0, 0)),
                  pl.BlockSpec((1, D_POOL), lambda b: (0, 0))],
        out_specs=blk,
        out_shape=jax.ShapeDtypeStruct((B, S, D_POOL), BF16),
        compiler_params=_params(("arbitrary",)),
        name="pool_mixer",
    )(u, w_pool, b_pool, pool_scale)


def _outproj_kernel(a_ref, p_ref, x_ref, w_ref, g_ref, rw_ref, rb_ref, x1_ref, h_ref, lg_ref):
    x1 = (x_ref[...]
          + jnp.dot(a_ref[...], w_ref[0:D_ATTN, :], preferred_element_type=F32)
          + jnp.dot(p_ref[...], w_ref[D_ATTN:, :], preferred_element_type=F32))
    x1_ref[...] = x1
    ms = jnp.mean(x1 * x1, axis=-1, keepdims=True)
    h = x1 * lax.rsqrt(ms + RMS_EPS) * g_ref[...]
    _store_token_tiles(h_ref, h)
    lg_ref[...] = jnp.dot(h, rw_ref[...], preferred_element_type=F32,
                          precision=lax.Precision.HIGHEST) + rb_ref[...]


def _outproj(attn, pool, x2, w_out, g2, rw, rb):
    T = x2.shape[0]
    tm = TOKEN_TILE
    row = lambda i: (i, 0)
    fixed = lambda i: (0, 0)
    return pl.pallas_call(
        _outproj_kernel,
        grid=(T // tm,),
        in_specs=[pl.BlockSpec((tm, D_ATTN), row),
                  pl.BlockSpec((tm, D_POOL), row),
                  pl.BlockSpec((tm, D_MODEL), row),
                  pl.BlockSpec((D_MODEL, D_MODEL), fixed),
                  pl.BlockSpec((1, D_MODEL), fixed),
                  pl.BlockSpec((D_MODEL, N_EXPERTS), fixed),
                  pl.BlockSpec((1, N_EXPERTS), fixed)],
        out_specs=[pl.BlockSpec((tm, D_MODEL), row),
                   pl.BlockSpec((tm * TILE_ROWS, LANES), row),
                   pl.BlockSpec((tm, N_EXPERTS), row)],
        out_shape=[jax.ShapeDtypeStruct((T, D_MODEL), F32),
                   jax.ShapeDtypeStruct((T * TILE_ROWS, LANES), F32),
                   jax.ShapeDtypeStruct((T, N_EXPERTS), F32)],
        compiler_params=_params(("arbitrary",)),
        name="outproj_router",
    )(attn, pool, x2, w_out, g2, rw, rb)


def _gather_kernel(idx_ref, src_ref, out_ref, sem):
    tg = out_ref.shape[0]
    for r in range(tg):
        pltpu.make_async_copy(src_ref.at[idx_ref[0, 0, r]], out_ref.at[r], sem).start(priority=r % 2)
    pltpu.make_async_copy(src_ref.at[pl.ds(0, tg)], out_ref, sem).wait()


def _gather_rows(src, idx):
    M = idx.shape[0]
    tg = GATHER_TILE
    assert M % tg == 0
    return pl.pallas_call(
        _gather_kernel,
        grid=(M // tg,),
        in_specs=[pl.BlockSpec((1, 1, tg), lambda i: (i, 0, 0), memory_space=pltpu.SMEM),
                  pl.BlockSpec(memory_space=pl.ANY)],
        out_specs=pl.BlockSpec((tg, TILE_ROWS, LANES), lambda i: (i, 0, 0)),
        out_shape=jax.ShapeDtypeStruct((M, TILE_ROWS, LANES), src.dtype),
        scratch_shapes=[pltpu.SemaphoreType.DMA(())],
        compiler_params=_params(("arbitrary",)),
        name="row_gather",
    )(idx.reshape(M // tg, 1, tg), src)


def _moe_kernel(be_ref, nu_ref, xs_ref, sg_ref, wg_ref, bg_ref, wu_ref, bu_ref, wd_ref, bd_ref,
                y_ref, wg_s, wu_s, wd_s):
    i = pl.program_id(0)
    e = be_ref[i]
    prev = be_ref[jnp.maximum(i - 1, 0)]

    @pl.when(jnp.logical_or(i == 0, e != prev))
    def _():
        rows = 128
        for c in range(D_MODEL // rows):
            sl = slice(c * rows, (c + 1) * rows)
            wg_s[sl, :] = wg_ref[sl, :].astype(BF16)
            wu_s[sl, :] = wu_ref[sl, :].astype(BF16)
            wd_s[sl, :] = wd_ref[sl, :].astype(BF16)

    tm = sg_ref.shape[0]

    @pl.when(i < nu_ref[0])
    def _():
        x = _load_token_tiles(xs_ref, tm, BF16)
        y = None
        for n in range(D_FF // FF_CHUNK):
            cols = slice(n * FF_CHUNK, (n + 1) * FF_CHUNK)
            g = jnp.dot(x, wg_s[:, cols], preferred_element_type=F32) + bg_ref[:, cols]
            g = jnp.minimum(g, SWIGLU_LIMIT)
            u = jnp.dot(x, wu_s[:, cols], preferred_element_type=F32) + bu_ref[:, cols]
            u = jnp.clip(u, -SWIGLU_LIMIT, SWIGLU_LIMIT)
            act = (g * jax.nn.sigmoid(SWIGLU_ALPHA * g) * (u + 1.0)).astype(BF16)
            part = jnp.dot(act, wd_s[cols, :], preferred_element_type=F32)
            y = part if y is None else y + part
        _store_token_tiles(y_ref, (y + bd_ref[...]) * sg_ref[...])

    @pl.when(i >= nu_ref[0])
    def _():
        y_ref[...] = jnp.zeros(y_ref.shape, y_ref.dtype)


def _moe_experts(block_expert, n_used, xs, slot_gate, wg, bg, wu, bu, wd, bd):
    P = xs.shape[0] // TILE_ROWS
    tm = MOE_TILE
    n_blocks = P // tm
    row = lambda i, be, nu: (i, 0)
    tiles = pl.BlockSpec((tm * TILE_ROWS, LANES), row)
    wmap = lambda i, be, nu: (be[i], 0, 0)
    wspec = pl.BlockSpec((None, D_MODEL, D_FF), wmap)
    bspec = pl.BlockSpec((None, 1, D_FF), wmap)
    grid_spec = pltpu.PrefetchScalarGridSpec(
        num_scalar_prefetch=2,
        grid=(n_blocks,),
        in_specs=[tiles,
                  pl.BlockSpec((tm, 1), row),
                  wspec, bspec, wspec, bspec, wspec, bspec],
        out_specs=tiles,
        scratch_shapes=[pltpu.VMEM((D_MODEL, D_FF), BF16),
                        pltpu.VMEM((D_MODEL, D_FF), BF16),
                        pltpu.VMEM((D_FF, D_MODEL), BF16)],
    )
    return pl.pallas_call(
        _moe_kernel,
        grid_spec=grid_spec,
        out_shape=jax.ShapeDtypeStruct((P * TILE_ROWS, LANES), F32),
        compiler_params=_params(("arbitrary",)),
        name="moe_experts",
    )(block_expert, n_used, xs, slot_gate, wg, bg, wu, bu, wd, bd)


def _combine_kernel(x1_ref, y_ref, g_ref, o_ref):
    x = x1_ref[...]
    for k in range(TOP_K):
        x = x + _load_token_tiles(y_ref.at[k], x.shape[0])
    ms = jnp.mean(x * x, axis=-1, keepdims=True)
    o_ref[...] = x * lax.rsqrt(ms + RMS_EPS) * g_ref[...]


def _combine(x1, yk, g):
    T = x1.shape[0]
    tm = TOKEN_TILE
    return pl.pallas_call(
        _combine_kernel,
        grid=(T // tm,),
        in_specs=[pl.BlockSpec((tm, D_MODEL), lambda i: (i, 0)),
                  pl.BlockSpec((TOP_K, tm * TILE_ROWS, LANES), lambda i: (0, i, 0)),
                  pl.BlockSpec((1, D_MODEL), lambda i: (0, 0))],
        out_specs=pl.BlockSpec((tm, D_MODEL), lambda i: (i, 0)),
        out_shape=jax.ShapeDtypeStruct((T, D_MODEL), F32),
        compiler_params=_params(("arbitrary",)),
        name="combine_norm",
    )(x1, yk, g)


def _route(logits):
    T = logits.shape[0]
    A = T * TOP_K
    tm = MOE_TILE
    P = -(-(A + N_EXPERTS * (tm - 1)) // tm) * tm
    n_blocks = P // tm
    top_vals, top_idx = lax.top_k(logits, TOP_K)
    gates = jax.nn.softmax(top_vals, axis=-1)
    flat_e = top_idx.reshape(A).astype(jnp.int32)
    order = jnp.argsort(flat_e).astype(jnp.int32)
    sorted_e = flat_e[order]
    counts = jnp.bincount(flat_e, length=N_EXPERTS).astype(jnp.int32)
    padded = ((counts + tm - 1) // tm) * tm
    group_start = jnp.cumsum(counts) - counts
    padded_end = jnp.cumsum(padded)
    padded_start = padded_end - padded
    dest = padded_start[sorted_e] + (jnp.arange(A, dtype=jnp.int32) - group_start[sorted_e])
    slot_tok = jnp.zeros((P,), jnp.int32).at[dest].set(order // TOP_K)
    slot_gate = jnp.zeros((P,), F32).at[dest].set(gates.reshape(A)[order])
    pos = jnp.zeros((A,), jnp.int32).at[order].set(dest)
    block_expert = jnp.clip(
        jnp.searchsorted(padded_end, jnp.arange(n_blocks, dtype=jnp.int32) * tm, side='right'),
        0, N_EXPERTS - 1).astype(jnp.int32)
    n_used = (padded_end[-1:] // tm).astype(jnp.int32)
    return slot_tok, slot_gate, pos, block_expert, n_used


def kernel(x, norm1_g, w_in, lambda_q1, lambda_k1, lambda_q2, lambda_k2, subln_g, w_pool, b_pool,
           pool_scale, w_out, norm2_g, router_w, router_b, w_gate, b_gate, w_up, b_up, w_down,
           b_down, final_g):
    B, S, D = x.shape
    T = B * S
    l = 0
    lambda_init = 0.8 - 0.6 * math.exp(-0.3 * l)
    x2 = x.reshape(T, D)

    w = w_in[l]
    wqv_t = jnp.concatenate([w[:, :D_ATTN], w[:, 2 * D_ATTN:3 * D_ATTN]], axis=1).T.astype(BF16)
    qt, k, vt, u = _inproj(x2, norm1_g[l][None, :], wqv_t, w[:, D_ATTN:2 * D_ATTN].astype(BF16),
                           w[:, 3 * D_ATTN:].astype(BF16), B, S)
    attn = _attention(qt, k.reshape(B, S, D_ATTN), vt,
                      lambda_q1[l][None, :], lambda_k1[l][None, :],
                      lambda_q2[l][None, :], lambda_k2[l][None, :],
                      subln_g[l][None, :], lambda_init)
    pool = _pool(u.reshape(B, S, D_POOL), w_pool[l].astype(BF16),
                 b_pool[l].reshape(1, D_POOL), pool_scale[l][None, :])
    x1, h2, logits = _outproj(attn.reshape(T, D_ATTN), pool.reshape(T, D_POOL), x2,
                              w_out[l].astype(BF16), norm2_g[l][None, :],
                              router_w[l], router_b[l][None, :])

    slot_tok, slot_gate, pos, block_expert, n_used = _route(logits)
    xs = _gather_rows(h2.reshape(T, TILE_ROWS, LANES), slot_tok)
    ys = _moe_experts(block_expert, n_used, xs.reshape(-1, LANES), slot_gate[:, None],
                      w_gate[l], b_gate[l][:, None, :], w_up[l], b_up[l][:, None, :],
                      w_down[l], b_down[l][:, None, :])
    yk = _gather_rows(ys.reshape(-1, TILE_ROWS, LANES), pos.reshape(T, TOP_K).T.reshape(T * TOP_K))
    out = _combine(x1, yk.reshape(TOP_K, T * TILE_ROWS, LANES), final_g[None, :])
    return out.reshape(B, S, D)
```

```python
import functools
import math

import jax
import jax.numpy as jnp
from jax import lax
from jax.experimental import pallas as pl
from jax.experimental.pallas import tpu as pltpu

D_MODEL = 1024
D_ATTN = 512
D_POOL = 512
N_DIFF_HEADS = 4
DIFF_QK_DIM = 64
DIFF_V_DIM = 128
POOL_WINDOWS = (2, 4, 8, 16)
POOL_GROUP_DIM = 128
N_EXPERTS = 32
TOP_K = 4
D_FF = 1024
SWIGLU_LIMIT = 7.0
SWIGLU_ALPHA = 1.702
RMS_EPS = 1e-5

F32 = jnp.float32
BF16 = jnp.bfloat16

TOKEN_TILE = 512
ATTN_TILE = 256
MOE_TILE = 512
GATHER_TILE = 256
FF_CHUNK = 512
VMEM_LIMIT = 56 * 1024 * 1024


def _params(sem, vmem=VMEM_LIMIT):
    return pltpu.CompilerParams(dimension_semantics=sem, vmem_limit_bytes=vmem)


LANES = 128
TILE_ROWS = D_MODEL // LANES
assert TILE_ROWS == 8


def _store_token_tiles(ref, x):
    n = x.shape[0]
    for c in range(TILE_ROWS):
        ref[pl.ds(c, n, stride=TILE_ROWS), :] = x[:, c * LANES:(c + 1) * LANES]


def _load_token_tiles(ref, n, dtype=None):
    cols = [ref[pl.ds(c, n, stride=TILE_ROWS), :] for c in range(TILE_ROWS)]
    if dtype is not None:
        cols = [c.astype(dtype) for c in cols]
    return jnp.concatenate(cols, axis=1)


def _inproj_kernel(x_ref, g_ref, wqv_ref, wk_ref, wu_ref, qt_ref, k_ref, vt_ref, u_ref):
    x = x_ref[...]
    ms = jnp.mean(x * x, axis=-1, keepdims=True)
    h = (x * lax.rsqrt(ms + RMS_EPS) * g_ref[...]).astype(BF16)
    qvt = lax.dot_general(wqv_ref[...], h, (((1,), (1,)), ((), ())), preferred_element_type=F32)
    qt_ref[...] = (qvt[:D_ATTN] * (DIFF_QK_DIM ** -0.5)).astype(BF16)
    vt_ref[...] = qvt[D_ATTN:].astype(BF16)
    k_ref[...] = jnp.dot(h, wk_ref[...], preferred_element_type=F32).astype(BF16)
    u_ref[...] = jnp.dot(h, wu_ref[...], preferred_element_type=F32)


def _inproj(x2, g, wqv_t, wk, wu, B, S):
    T = x2.shape[0]
    tm = TOKEN_TILE
    per_seq = S // tm
    row = lambda i: (i, 0)
    fixed = lambda i: (0, 0)
    tmap = lambda i: (i // per_seq, 0, i % per_seq)
    return pl.pallas_call(
        _inproj_kernel,
        grid=(T // tm,),
        in_specs=[pl.BlockSpec((tm, D_MODEL), row),
                  pl.BlockSpec((1, D_MODEL), fixed),
                  pl.BlockSpec((2 * D_ATTN, D_MODEL), fixed),
                  pl.BlockSpec((D_MODEL, D_ATTN), fixed),
                  pl.BlockSpec((D_MODEL, D_POOL), fixed)],
        out_specs=[pl.BlockSpec((None, D_ATTN, tm), tmap),
                   pl.BlockSpec((tm, D_ATTN), row),
                   pl.BlockSpec((None, D_ATTN, tm), tmap),
                   pl.BlockSpec((tm, D_POOL), row)],
        out_shape=[jax.ShapeDtypeStruct((B, D_ATTN, S), BF16),
                   jax.ShapeDtypeStruct((T, D_ATTN), BF16),
                   jax.ShapeDtypeStruct((B, D_ATTN, S), BF16),
                   jax.ShapeDtypeStruct((T, D_POOL), F32)],
        compiler_params=_params(("arbitrary",)),
        name="inproj",
    )(x2, g, wqv_t, wk, wu)


def _attn_kernel(qa_ref, qb_ref, k_ref, vt_ref, lq1_ref, lk1_ref, lq2_ref, lk2_ref, sg_ref,
                 o_ref, qs_s, vt_s, m_s, acc_s, *, lambda_init, n_q):
    tq = qa_ref.shape[1]
    tk = tq
    dv = DIFF_V_DIM
    p = pl.program_id(2)

    @pl.when(p == 0)
    def _():
        for c in range(n_q):
            vt_s[c, 0:dv, :] = vt_ref[:, c * tk:(c + 1) * tk]
            vt_s[c, dv:, :] = jnp.ones((vt_s.shape[1] - dv, tk), BF16)

    feat = lax.broadcasted_iota(jnp.int32, (dv, tq), 0)
    for blk, q_ref in enumerate((qa_ref, qb_ref)):
        qt = q_ref[...]
        zero = jnp.zeros_like(qt)
        qs_s[blk] = jnp.concatenate([jnp.where(feat < DIFF_QK_DIM, qt, zero),
                                     jnp.where(feat >= DIFF_QK_DIM, qt, zero)], axis=1)
    m_s[...] = jnp.full(m_s.shape, -jnp.inf, F32)
    acc_s[...] = jnp.zeros(acc_s.shape, F32)

    key = lax.broadcasted_iota(jnp.int32, (tk, tq), 0)
    qry = lax.broadcasted_iota(jnp.int32, (tk, tq), 1)
    bias = jnp.where(key <= qry, 0.0, -jnp.inf).astype(F32)
    bias = jnp.concatenate([bias, bias], axis=1)

    items = [(0, p, True), (1, n_q - 1 - p, True)]
    for n in range(n_q - 1):
        in_a = n < p
        items.append((jnp.where(in_a, 0, 1), jnp.where(in_a, n, n - p), False))

    def scores(blk, chunk, diag):
        kc = k_ref[pl.ds(pl.multiple_of(chunk * tk, tk), tk), :]
        s = jnp.dot(kc, qs_s[blk], preferred_element_type=F32)
        return s + bias if diag else s

    for blk, chunk, diag in items:
        s = scores(blk, chunk, diag)
        m_s[blk] = jnp.maximum(m_s[blk], jnp.max(s, axis=0, keepdims=True))

    for blk, chunk, diag in items:
        s = scores(blk, chunk, diag)
        pt = jnp.exp(s - m_s[blk]).astype(BF16)
        acc_s[blk] += jnp.dot(vt_s[chunk], pt, preferred_element_type=F32)

    lam = (jnp.exp(jnp.sum(lq1_ref[...] * lk1_ref[...]))
           - jnp.exp(jnp.sum(lq2_ref[...] * lk2_ref[...])) + lambda_init)
    for blk, qblock in enumerate((p, n_q - 1 - p)):
        acc = acc_s[blk]
        o0 = acc[0:dv, 0:tq] / acc[dv:dv + 1, 0:tq]
        o1 = acc[0:dv, tq:] / acc[dv:dv + 1, tq:]
        a = o0 - lam * o1
        ms = jnp.mean(a * a, axis=0, keepdims=True)
        y = (a * lax.rsqrt(ms + RMS_EPS)).T * sg_ref[...]
        o_ref[pl.ds(pl.multiple_of(qblock * tq, tq), tq), :] = (y * (1.0 - lambda_init)).astype(o_ref.dtype)


def _attention(qt, k, vt, lq1, lk1, lq2, lk2, subln_g, lambda_init):
    B, _, S = qt.shape
    tq = ATTN_TILE
    n_q = S // tq
    assert S % tq == 0 and n_q % 2 == 0
    ones_rows = 16
    vec = lambda n: pl.BlockSpec((1, n), lambda b, h, p: (0, 0))
    return pl.pallas_call(
        functools.partial(_attn_kernel, lambda_init=lambda_init, n_q=n_q),
        grid=(B, N_DIFF_HEADS, n_q // 2),
        in_specs=[pl.BlockSpec((None, DIFF_V_DIM, tq), lambda b, h, p: (b, h, p)),
                  pl.BlockSpec((None, DIFF_V_DIM, tq), lambda b, h, p: (b, h, n_q - 1 - p)),
                  pl.BlockSpec((None, S, DIFF_V_DIM), lambda b, h, p: (b, 0, h)),
                  pl.BlockSpec((None, DIFF_V_DIM, S), lambda b, h, p: (b, h, 0)),
                  vec(DIFF_QK_DIM), vec(DIFF_QK_DIM), vec(DIFF_QK_DIM), vec(DIFF_QK_DIM),
                  vec(DIFF_V_DIM)],
        out_specs=pl.BlockSpec((None, S, DIFF_V_DIM), lambda b, h, p: (b, 0, h)),
        out_shape=jax.ShapeDtypeStruct((B, S, D_ATTN), BF16),
        scratch_shapes=[pltpu.VMEM((2, DIFF_V_DIM, 2 * tq), BF16),
                        pltpu.VMEM((n_q, DIFF_V_DIM + ones_rows, tq), BF16),
                        pltpu.VMEM((2, 1, 2 * tq), F32),
                        pltpu.VMEM((2, DIFF_V_DIM + ones_rows, 2 * tq), F32)],
        compiler_params=_params(("arbitrary", "arbitrary", "arbitrary")),
        name="diff_attn",
    )(qt, qt, k, vt, lq1, lk1, lq2, lk2, subln_g)


def _pool_kernel(u_ref, w_ref, b_ref, sc_ref, o_ref):
    S = u_ref.shape[0]
    row = lax.broadcasted_iota(jnp.int32, (S, POOL_GROUP_DIM), 0)
    for g, w in enumerate(POOL_WINDOWS):
        cols = slice(g * POOL_GROUP_DIM, (g + 1) * POOL_GROUP_DIM)
        ug = u_ref[:, cols]
        s = ug
        span = 1
        while span < w:
            shifted = pltpu.roll(s, shift=span, axis=0)
            s = s + jnp.where(row >= span, shifted, 0.0)
            span *= 2
        cnt = jnp.minimum(row + 1, w).astype(F32)
        z = (s / cnt - ug).astype(BF16)
        y = jnp.dot(z, w_ref[g], preferred_element_type=F32) + b_ref[:, cols]
        o_ref[:, cols] = (y * sc_ref[:, cols]).astype(o_ref.dtype)


def _pool(u, w_pool, b_pool, pool_scale):
    B, S, _ = u.shape
    blk = pl.BlockSpec((None, S, D_POOL), lambda b: (b, 0, 0))
    return pl.pallas_call(
        _pool_kernel,
        grid=(B,),
        in_specs=[blk,
                  pl.BlockSpec((len(POOL_WINDOWS), POOL_GROUP_DIM, POOL_GROUP_DIM), lambda b: (0, 0, 0)),
                  pl.BlockSpec((1, D_POOL), lambda b: (0, 0)),
                  pl.BlockSpec((1, D_POOL), lambda b: (0, 0))],
        out_specs=blk,
        out_shape=jax.ShapeDtypeStruct((B, S, D_POOL), BF16),
        compiler_params=_params(("arbitrary",)),
        name="pool_mixer",
    )(u, w_pool, b_pool, pool_scale)


def _outproj_kernel(a_ref, p_ref, x_ref, w_ref, g_ref, rw_ref, rb_ref, x1_ref, h_ref, lg_ref):
    x1 = (x_ref[...]
          + jnp.dot(a_ref[...], w_ref[0:D_ATTN, :], preferred_element_type=F32)
          + jnp.dot(p_ref[...], w_ref[D_ATTN:, :], preferred_element_type=F32))
    x1_ref[...] = x1
    ms = jnp.mean(x1 * x1, axis=-1, keepdims=True)
    h = x1 * lax.rsqrt(ms + RMS_EPS) * g_ref[...]
    _store_token_tiles(h_ref, h)
    lg = jnp.dot(h, rw_ref[...], preferred_element_type=F32,
                 precision=lax.Precision.HIGHEST) + rb_ref[...]
    lg_ref[...] = lg.T[:N_EXPERTS]


def _outproj(attn, pool, x2, w_out, g2, rw, rb):
    T = x2.shape[0]
    tm = TOKEN_TILE
    row = lambda i: (i, 0)
    fixed = lambda i: (0, 0)
    rw = jnp.pad(rw, ((0, 0), (0, LANES - N_EXPERTS)))
    rb = jnp.pad(rb, ((0, 0), (0, LANES - N_EXPERTS)))
    return pl.pallas_call(
        _outproj_kernel,
        grid=(T // tm,),
        in_specs=[pl.BlockSpec((tm, D_ATTN), row),
                  pl.BlockSpec((tm, D_POOL), row),
                  pl.BlockSpec((tm, D_MODEL), row),
                  pl.BlockSpec((D_MODEL, D_MODEL), fixed),
                  pl.BlockSpec((1, D_MODEL), fixed),
                  pl.BlockSpec((D_MODEL, LANES), fixed),
                  pl.BlockSpec((1, LANES), fixed)],
        out_specs=[pl.BlockSpec((tm, D_MODEL), row),
                   pl.BlockSpec((tm * TILE_ROWS, LANES), row),
                   pl.BlockSpec((None, N_EXPERTS, tm), lambda i: (i, 0, 0))],
        out_shape=[jax.ShapeDtypeStruct((T, D_MODEL), F32),
                   jax.ShapeDtypeStruct((T * TILE_ROWS, LANES), F32),
                   jax.ShapeDtypeStruct((T // tm, N_EXPERTS, tm), F32)],
        compiler_params=_params(("arbitrary",)),
        name="outproj_router",
    )(attn, pool, x2, w_out, g2, rw, rb)


def _route_kernel(lt_ref, pos_ref, gate_ref, cnt_ref, idx_s, rank_s):
    nt, E, W = lt_ref.shape
    e_iota = lax.broadcasted_iota(jnp.int32, (E, W), 0)
    before = (lax.broadcasted_iota(jnp.int32, (W, W), 0)
              < lax.broadcasted_iota(jnp.int32, (W, W), 1)).astype(BF16)
    ones = jnp.ones((W, LANES), BF16)
    widen = lambda a: jnp.concatenate([a] * (W // LANES), axis=1)

    def phase1(i, running):
        v = lt_ref[i]
        sel = jnp.zeros((E, W), F32)
        tops, hots = [], []
        for k in range(TOP_K):
            m = jnp.max(v, axis=0, keepdims=True)
            idx = jnp.min(jnp.where(v == m, e_iota, E), axis=0, keepdims=True)
            hot = e_iota == idx
            v = jnp.where(hot, -jnp.inf, v)
            sel = sel + hot.astype(F32)
            idx_s[i, k:k + 1, :] = idx
            tops.append(m)
            hots.append(hot)
        selb = sel.astype(BF16)
        rank = jnp.dot(selb, before, preferred_element_type=F32) + widen(running)
        for k in range(TOP_K):
            rank_s[i, k:k + 1, :] = jnp.sum(jnp.where(hots[k], rank, 0.0), axis=0, keepdims=True)
        ex = [jnp.exp(t - tops[0]) for t in tops]
        den = ex[0] + ex[1] + ex[2] + ex[3]
        for k in range(TOP_K):
            gate_ref[i, k:k + 1, :] = ex[k] / den
        return running + jnp.dot(selb, ones, preferred_element_type=F32)

    counts = lax.fori_loop(0, nt, phase1, jnp.zeros((E, LANES), F32))
    cnt_ref[...] = counts.astype(jnp.int32)

    row = lax.broadcasted_iota(jnp.int32, (E, LANES), 0)
    incl = counts
    span = 1
    while span < E:
        incl = incl + jnp.where(row >= span, pltpu.roll(incl, shift=span, axis=0), 0.0)
        span *= 2
    start = widen(incl - counts)

    def phase2(i, c):
        for k in range(TOP_K):
            hot = e_iota == idx_s[i, k:k + 1, :]
            base = jnp.sum(jnp.where(hot, start, 0.0), axis=0, keepdims=True)
            pos_ref[i, k:k + 1, :] = (base + rank_s[i, k:k + 1, :]).astype(jnp.int32)
        return c

    lax.fori_loop(0, nt, phase2, 0)


def _route(logits_t):
    nt, E, W = logits_t.shape
    whole = lambda shape: pl.BlockSpec(shape, lambda: (0,) * len(shape))
    return pl.pallas_call(
        _route_kernel,
        in_specs=[whole((nt, E, W))],
        out_specs=[whole((nt, TOP_K, W)), whole((nt, TOP_K, W)), whole((E, LANES))],
        out_shape=[jax.ShapeDtypeStruct((nt, TOP_K, W), jnp.int32),
                   jax.ShapeDtypeStruct((nt, TOP_K, W), F32),
                   jax.ShapeDtypeStruct((E, LANES), jnp.int32)],
        scratch_shapes=[pltpu.VMEM((nt, TOP_K, W), jnp.int32),
                        pltpu.VMEM((nt, TOP_K, W), F32)],
        compiler_params=pltpu.CompilerParams(vmem_limit_bytes=VMEM_LIMIT),
        name="route",
    )(logits_t)


def _work_items(counts):
    tm = MOE_TILE
    ends = jnp.cumsum(counts)
    starts = ends - counts
    first_blk = starts // tm
    last_blk = jnp.where(counts > 0, (ends - 1) // tm, first_blk - 1)
    n_items = last_blk - first_blk + 1
    item_end = jnp.cumsum(n_items)
    item_start = item_end - n_items
    return starts, ends, first_blk, item_start, item_end


def _item_table(counts, n_blocks):
    tm = MOE_TILE
    starts, ends, first_blk, item_start, item_end = _work_items(counts)
    n_slots = n_blocks + N_EXPERTS - 1
    j = jnp.arange(n_slots, dtype=jnp.int32)
    e = jnp.minimum(jnp.searchsorted(item_end, j, side='right'), N_EXPERTS - 1).astype(jnp.int32)
    real = j < item_end[-1]
    blk = jnp.where(real, first_blk[e] + j - item_start[e], n_blocks - 1)
    lo = jnp.where(real, jnp.maximum(starts[e], blk * tm) - blk * tm, 0)
    hi = jnp.where(real, jnp.minimum(ends[e], (blk + 1) * tm) - blk * tm, 0)
    last_e = e[jnp.maximum(item_end[-1] - 1, 0)]
    e = jnp.where(real, e, last_e)
    as_i32 = lambda a: a.astype(jnp.int32)
    return as_i32(blk), as_i32(e), as_i32(lo), as_i32(hi)


def _scatter_kernel(pos_ref, src_ref, dst_ref, sem):
    tg = src_ref.shape[0]
    for r in range(tg):
        for k in range(TOP_K):
            n = r * TOP_K + k
            pltpu.make_async_copy(src_ref.at[r], dst_ref.at[pos_ref[0, 0, n]], sem).start(priority=n % 2)
    for k in range(TOP_K):
        pltpu.make_async_copy(src_ref, dst_ref.at[pl.ds(0, tg)], sem).wait()


def _scatter_rows(src, pos):
    T = src.shape[0]
    tg = GATHER_TILE
    return pl.pallas_call(
        _scatter_kernel,
        grid=(T // tg,),
        in_specs=[pl.BlockSpec((1, 1, tg * TOP_K), lambda i: (i, 0, 0), memory_space=pltpu.SMEM),
                  pl.BlockSpec((tg, TILE_ROWS, LANES), lambda i: (i, 0, 0))],
        out_specs=pl.BlockSpec(memory_space=pl.ANY),
        out_shape=jax.ShapeDtypeStruct((T * TOP_K, TILE_ROWS, LANES), src.dtype),
        scratch_shapes=[pltpu.SemaphoreType.DMA(())],
        compiler_params=_params(("arbitrary",)),
        name="row_scatter",
    )(pos.reshape(T // tg, 1, tg * TOP_K), src)


def _gather_kernel(idx_ref, src_ref, out_ref, sem):
    tg = out_ref.shape[0]
    for r in range(tg):
        pltpu.make_async_copy(src_ref.at[idx_ref[0, 0, r]], out_ref.at[r], sem).start(priority=r % 2)
    pltpu.make_async_copy(src_ref.at[pl.ds(0, tg)], out_ref, sem).wait()


def _gather_rows(src, idx):
    M = idx.shape[0]
    tg = GATHER_TILE
    assert M % tg == 0
    return pl.pallas_call(
        _gather_kernel,
        grid=(M // tg,),
        in_specs=[pl.BlockSpec((1, 1, tg), lambda i: (i, 0, 0), memory_space=pltpu.SMEM),
                  pl.BlockSpec(memory_space=pl.ANY)],
        out_specs=pl.BlockSpec((tg, TILE_ROWS, LANES), lambda i: (i, 0, 0)),
        out_shape=jax.ShapeDtypeStruct((M, TILE_ROWS, LANES), src.dtype),
        scratch_shapes=[pltpu.SemaphoreType.DMA(())],
        compiler_params=_params(("arbitrary",)),
        name="row_gather",
    )(idx.reshape(M // tg, 1, tg), src)


def _moe_kernel(blk_ref, e_ref, lo_ref, hi_ref, xs_ref, wg_ref, bg_ref, wu_ref, bu_ref, wd_ref,
                bd_ref, y_ref, wg_s, wu_s, wd_s):
    j = pl.program_id(0)
    jp = jnp.maximum(j - 1, 0)
    tm = MOE_TILE

    @pl.when(jnp.logical_or(j == 0, e_ref[j] != e_ref[jp]))
    def _():
        rows = 128
        for c in range(D_MODEL // rows):
            sl = slice(c * rows, (c + 1) * rows)
            wg_s[sl, :] = wg_ref[sl, :].astype(BF16)
            wu_s[sl, :] = wu_ref[sl, :].astype(BF16)
            wd_s[sl, :] = wd_ref[sl, :].astype(BF16)

    lo = lo_ref[j]
    hi = hi_ref[j]

    @pl.when(lo < hi)
    def _():
        x = _load_token_tiles(xs_ref, tm, BF16)
        y = None
        for n in range(D_FF // FF_CHUNK):
            cols = slice(n * FF_CHUNK, (n + 1) * FF_CHUNK)
            g = jnp.dot(x, wg_s[:, cols], preferred_element_type=F32) + bg_ref[:, cols]
            g = jnp.minimum(g, SWIGLU_LIMIT)
            u = jnp.dot(x, wu_s[:, cols], preferred_element_type=F32) + bu_ref[:, cols]
            u = jnp.clip(u, -SWIGLU_LIMIT, SWIGLU_LIMIT)
            act = (g * jax.nn.sigmoid(SWIGLU_ALPHA * g) * (u + 1.0)).astype(BF16)
            part = jnp.dot(act, wd_s[cols, :], preferred_element_type=F32)
            y = part if y is None else y + part
        y = y + bd_ref[...]
        rows = lax.broadcasted_iota(jnp.int32, (tm, LANES), 0)
        mine = jnp.logical_and(rows >= lo, rows < hi)
        first_visit = jnp.logical_or(j == 0, blk_ref[j] != blk_ref[jp])

        @pl.when(first_visit)
        def _():
            for c in range(TILE_ROWS):
                y_ref[pl.ds(c, tm, stride=TILE_ROWS), :] = jnp.where(
                    mine, y[:, c * LANES:(c + 1) * LANES], 0.0)

        @pl.when(jnp.logical_not(first_visit))
        def _():
            for c in range(TILE_ROWS):
                sl = pl.ds(c, tm, stride=TILE_ROWS)
                y_ref[sl, :] = jnp.where(mine, y[:, c * LANES:(c + 1) * LANES], y_ref[sl, :])


def _moe_experts(items, xs, wg, bg, wu, bu, wd, bd):
    P = xs.shape[0] // TILE_ROWS
    tm = MOE_TILE
    n_items = items[0].shape[0]
    tiles = pl.BlockSpec((tm * TILE_ROWS, LANES), lambda j, blk, e, lo, hi: (blk[j], 0))
    wmap = lambda j, blk, e, lo, hi: (e[j], 0, 0)
    wspec = pl.BlockSpec((None, D_MODEL, D_FF), wmap)
    bspec = pl.BlockSpec((None, 1, D_FF), wmap)
    grid_spec = pltpu.PrefetchScalarGridSpec(
        num_scalar_prefetch=4,
        grid=(n_items,),
        in_specs=[tiles, wspec, bspec, wspec, bspec, wspec, bspec],
        out_specs=tiles,
        scratch_shapes=[pltpu.VMEM((D_MODEL, D_FF), BF16),
                        pltpu.VMEM((D_MODEL, D_FF), BF16),
                        pltpu.VMEM((D_FF, D_MODEL), BF16)],
    )
    return pl.pallas_call(
        _moe_kernel,
        grid_spec=grid_spec,
        out_shape=jax.ShapeDtypeStruct((P * TILE_ROWS, LANES), F32),
        compiler_params=_params(("arbitrary",)),
        name="moe_experts",
    )(*items, xs, wg, bg, wu, bu, wd, bd)


def _combine_kernel(x1_ref, y_ref, gt_ref, g_ref, o_ref):
    x = x1_ref[...]
    gates = gt_ref[...]
    for k in range(TOP_K):
        x = x + gates[:, k:k + 1] * _load_token_tiles(y_ref.at[k], x.shape[0])
    ms = jnp.mean(x * x, axis=-1, keepdims=True)
    o_ref[...] = x * lax.rsqrt(ms + RMS_EPS) * g_ref[...]


def _combine(x1, yk, gates, g):
    T = x1.shape[0]
    tm = TOKEN_TILE
    return pl.pallas_call(
        _combine_kernel,
        grid=(T // tm,),
        in_specs=[pl.BlockSpec((tm, D_MODEL), lambda i: (i, 0)),
                  pl.BlockSpec((TOP_K, tm * TILE_ROWS, LANES), lambda i: (0, i, 0)),
                  pl.BlockSpec((tm, TOP_K), lambda i: (i, 0)),
                  pl.BlockSpec((1, D_MODEL), lambda i: (0, 0))],
        out_specs=pl.BlockSpec((tm, D_MODEL), lambda i: (i, 0)),
        out_shape=jax.ShapeDtypeStruct((T, D_MODEL), F32),
        compiler_params=_params(("arbitrary",)),
        name="combine_norm",
    )(x1, yk, gates, g)


def kernel(x, norm1_g, w_in, lambda_q1, lambda_k1, lambda_q2, lambda_k2, subln_g, w_pool, b_pool,
           pool_scale, w_out, norm2_g, router_w, router_b, w_gate, b_gate, w_up, b_up, w_down,
           b_down, final_g):
    B, S, D = x.shape
    T = B * S
    l = 0
    lambda_init = 0.8 - 0.6 * math.exp(-0.3 * l)
    x2 = x.reshape(T, D)

    w = w_in[l]
    wqv_t = jnp.concatenate([w[:, :D_ATTN], w[:, 2 * D_ATTN:3 * D_ATTN]], axis=1).T.astype(BF16)
    qt, k, vt, u = _inproj(x2, norm1_g[l][None, :], wqv_t, w[:, D_ATTN:2 * D_ATTN].astype(BF16),
                           w[:, 3 * D_ATTN:].astype(BF16), B, S)
    attn = _attention(qt, k.reshape(B, S, D_ATTN), vt,
                      lambda_q1[l][None, :], lambda_k1[l][None, :],
                      lambda_q2[l][None, :], lambda_k2[l][None, :],
                      subln_g[l][None, :], lambda_init)
    pool = _pool(u.reshape(B, S, D_POOL), w_pool[l].astype(BF16),
                 b_pool[l].reshape(1, D_POOL), pool_scale[l][None, :])
    x1, h2, logits_t = _outproj(attn.reshape(T, D_ATTN), pool.reshape(T, D_POOL), x2,
                                w_out[l].astype(BF16), norm2_g[l][None, :],
                                router_w[l], router_b[l][None, :])

    pos_t, gates_t, counts = _route(logits_t)
    pos_tok = pos_t.transpose(0, 2, 1).reshape(T * TOP_K)
    pos_k = pos_t.transpose(1, 0, 2).reshape(TOP_K * T)
    gates = gates_t.transpose(0, 2, 1).reshape(T, TOP_K)
    assert (T * TOP_K) % MOE_TILE == 0
    items = _item_table(counts[:, 0], T * TOP_K // MOE_TILE)

    xs = _scatter_rows(h2.reshape(T, TILE_ROWS, LANES), pos_tok)
    ys = _moe_experts(items, xs.reshape(-1, LANES),
                      w_gate[l], b_gate[l][:, None, :], w_up[l], b_up[l][:, None, :],
                      w_down[l], b_down[l][:, None, :])
    yk = _gather_rows(ys.reshape(-1, TILE_ROWS, LANES), pos_k)
    out = _combine(x1, yk.reshape(TOP_K, T * TILE_ROWS, LANES), gates, final_g[None, :])
    return out.reshape(B, S, D)
```

```python
import functools
import math

import jax
import jax.numpy as jnp
from jax import lax
from jax.experimental import pallas as pl
from jax.experimental.pallas import tpu as pltpu

D_MODEL = 1024
D_ATTN = 512
D_POOL = 512
N_DIFF_HEADS = 4
DIFF_QK_DIM = 64
DIFF_V_DIM = 128
POOL_WINDOWS = (2, 4, 8, 16)
POOL_GROUP_DIM = 128
N_EXPERTS = 32
TOP_K = 4
D_FF = 1024
SWIGLU_LIMIT = 7.0
SWIGLU_ALPHA = 1.702
RMS_EPS = 1e-5

F32 = jnp.float32
BF16 = jnp.bfloat16

TOKEN_TILE = 512
ATTN_TILE = 256
MOE_TILE = 512
GATHER_TILE = 256
COMBINE_TILE = 256
FF_CHUNK = 512
VMEM_LIMIT = 56 * 1024 * 1024


def _params(sem, vmem=VMEM_LIMIT):
    return pltpu.CompilerParams(dimension_semantics=sem, vmem_limit_bytes=vmem)


LANES = 128
TILE_ROWS = D_MODEL // LANES
assert TILE_ROWS == 8


def _store_token_tiles(ref, x):
    n = x.shape[0]
    for c in range(TILE_ROWS):
        ref[pl.ds(c, n, stride=TILE_ROWS), :] = x[:, c * LANES:(c + 1) * LANES]


def _load_token_tiles(ref, n, dtype=None):
    cols = [ref[pl.ds(c, n, stride=TILE_ROWS), :] for c in range(TILE_ROWS)]
    if dtype is not None:
        cols = [c.astype(dtype) for c in cols]
    return jnp.concatenate(cols, axis=1)


def _inproj_kernel(x_ref, g_ref, wqv_ref, wk_ref, wu_ref, qt_ref, k_ref, vt_ref, u_ref):
    x = x_ref[...]
    ms = jnp.mean(x * x, axis=-1, keepdims=True)
    h = (x * lax.rsqrt(ms + RMS_EPS) * g_ref[...]).astype(BF16)
    qvt = lax.dot_general(wqv_ref[...], h, (((1,), (1,)), ((), ())), preferred_element_type=F32)
    qt_ref[...] = (qvt[:D_ATTN] * (DIFF_QK_DIM ** -0.5)).astype(BF16)
    vt_ref[...] = qvt[D_ATTN:].astype(BF16)
    k_ref[...] = jnp.dot(h, wk_ref[...], preferred_element_type=F32).astype(BF16)
    u_ref[...] = jnp.dot(h, wu_ref[...], preferred_element_type=F32)


def _inproj(x2, g, wqv_t, wk, wu, B, S):
    T = x2.shape[0]
    tm = TOKEN_TILE
    per_seq = S // tm
    row = lambda i: (i, 0)
    fixed = lambda i: (0, 0)
    tmap = lambda i: (i // per_seq, 0, i % per_seq)
    return pl.pallas_call(
        _inproj_kernel,
        grid=(T // tm,),
        in_specs=[pl.BlockSpec((tm, D_MODEL), row),
                  pl.BlockSpec((1, D_MODEL), fixed),
                  pl.BlockSpec((2 * D_ATTN, D_MODEL), fixed),
                  pl.BlockSpec((D_MODEL, D_ATTN), fixed),
                  pl.BlockSpec((D_MODEL, D_POOL), fixed)],
        out_specs=[pl.BlockSpec((None, D_ATTN, tm), tmap),
                   pl.BlockSpec((tm, D_ATTN), row),
                   pl.BlockSpec((None, D_ATTN, tm), tmap),
                   pl.BlockSpec((tm, D_POOL), row)],
        out_shape=[jax.ShapeDtypeStruct((B, D_ATTN, S), BF16),
                   jax.ShapeDtypeStruct((T, D_ATTN), BF16),
                   jax.ShapeDtypeStruct((B, D_ATTN, S), BF16),
                   jax.ShapeDtypeStruct((T, D_POOL), F32)],
        compiler_params=_params(("arbitrary",)),
        name="inproj",
    )(x2, g, wqv_t, wk, wu)


def _attn_kernel(qa_ref, qb_ref, k_ref, vt_ref, lq1_ref, lk1_ref, lq2_ref, lk2_ref, sg_ref,
                 o_ref, qs_s, vt_s, m_s, acc_s, *, lambda_init, n_q):
    tq = qa_ref.shape[1]
    tk = tq
    dv = DIFF_V_DIM
    p = pl.program_id(2)

    @pl.when(p == 0)
    def _():
        for c in range(n_q):
            vt_s[c, 0:dv, :] = vt_ref[:, c * tk:(c + 1) * tk]
            vt_s[c, dv:, :] = jnp.ones((vt_s.shape[1] - dv, tk), BF16)

    feat = lax.broadcasted_iota(jnp.int32, (dv, tq), 0)
    for blk, q_ref in enumerate((qa_ref, qb_ref)):
        qt = q_ref[...]
        zero = jnp.zeros_like(qt)
        qs_s[blk] = jnp.concatenate([jnp.where(feat < DIFF_QK_DIM, qt, zero),
                                     jnp.where(feat >= DIFF_QK_DIM, qt, zero)], axis=1)
    m_s[...] = jnp.full(m_s.shape, -jnp.inf, F32)
    acc_s[...] = jnp.zeros(acc_s.shape, F32)

    key = lax.broadcasted_iota(jnp.int32, (tk, tq), 0)
    qry = lax.broadcasted_iota(jnp.int32, (tk, tq), 1)
    bias = jnp.where(key <= qry, 0.0, -jnp.inf).astype(F32)
    bias = jnp.concatenate([bias, bias], axis=1)

    items = [(0, p, True), (1, n_q - 1 - p, True)]
    for n in range(n_q - 1):
        in_a = n < p
        items.append((jnp.where(in_a, 0, 1), jnp.where(in_a, n, n - p), False))

    def scores(blk, chunk, diag):
        kc = k_ref[pl.ds(pl.multiple_of(chunk * tk, tk), tk), :]
        s = jnp.dot(kc, qs_s[blk], preferred_element_type=F32)
        return s + bias if diag else s

    for blk, chunk, diag in items:
        s = scores(blk, chunk, diag)
        m_s[blk] = jnp.maximum(m_s[blk], jnp.max(s, axis=0, keepdims=True))

    for blk, chunk, diag in items:
        s = scores(blk, chunk, diag)
        pt = jnp.exp(s - m_s[blk]).astype(BF16)
        acc_s[blk] += jnp.dot(vt_s[chunk], pt, preferred_element_type=F32)

    lam = (jnp.exp(jnp.sum(lq1_ref[...] * lk1_ref[...]))
           - jnp.exp(jnp.sum(lq2_ref[...] * lk2_ref[...])) + lambda_init)
    for blk, qblock in enumerate((p, n_q - 1 - p)):
        acc = acc_s[blk]
        o0 = acc[0:dv, 0:tq] / acc[dv:dv + 1, 0:tq]
        o1 = acc[0:dv, tq:] / acc[dv:dv + 1, tq:]
        a = o0 - lam * o1
        ms = jnp.mean(a * a, axis=0, keepdims=True)
        y = (a * lax.rsqrt(ms + RMS_EPS)).T * sg_ref[...]
        o_ref[pl.ds(pl.multiple_of(qblock * tq, tq), tq), :] = (y * (1.0 - lambda_init)).astype(o_ref.dtype)


def _attention(qt, k, vt, lq1, lk1, lq2, lk2, subln_g, lambda_init):
    B, _, S = qt.shape
    tq = ATTN_TILE
    n_q = S // tq
    assert S % tq == 0 and n_q % 2 == 0
    ones_rows = 16
    vec = lambda n: pl.BlockSpec((1, n), lambda b, h, p: (0, 0))
    return pl.pallas_call(
        functools.partial(_attn_kernel, lambda_init=lambda_init, n_q=n_q),
        grid=(B, N_DIFF_HEADS, n_q // 2),
        in_specs=[pl.BlockSpec((None, DIFF_V_DIM, tq), lambda b, h, p: (b, h, p)),
                  pl.BlockSpec((None, DIFF_V_DIM, tq), lambda b, h, p: (b, h, n_q - 1 - p)),
                  pl.BlockSpec((None, S, DIFF_V_DIM), lambda b, h, p: (b, 0, h)),
                  pl.BlockSpec((None, DIFF_V_DIM, S), lambda b, h, p: (b, h, 0)),
                  vec(DIFF_QK_DIM), vec(DIFF_QK_DIM), vec(DIFF_QK_DIM), vec(DIFF_QK_DIM),
                  vec(DIFF_V_DIM)],
        out_specs=pl.BlockSpec((None, S, DIFF_V_DIM), lambda b, h, p: (b, 0, h)),
        out_shape=jax.ShapeDtypeStruct((B, S, D_ATTN), BF16),
        scratch_shapes=[pltpu.VMEM((2, DIFF_V_DIM, 2 * tq), BF16),
                        pltpu.VMEM((n_q, DIFF_V_DIM + ones_rows, tq), BF16),
                        pltpu.VMEM((2, 1, 2 * tq), F32),
                        pltpu.VMEM((2, DIFF_V_DIM + ones_rows, 2 * tq), F32)],
        compiler_params=_params(("arbitrary", "arbitrary", "arbitrary")),
        name="diff_attn",
    )(qt, qt, k, vt, lq1, lk1, lq2, lk2, subln_g)


def _pool_kernel(u_ref, w_ref, b_ref, sc_ref, o_ref):
    S = u_ref.shape[0]
    row = lax.broadcasted_iota(jnp.int32, (S, POOL_GROUP_DIM), 0)
    for g, w in enumerate(POOL_WINDOWS):
        cols = slice(g * POOL_GROUP_DIM, (g + 1) * POOL_GROUP_DIM)
        ug = u_ref[:, cols]
        s = ug
        span = 1
        while span < w:
            shifted = pltpu.roll(s, shift=span, axis=0)
            s = s + jnp.where(row >= span, shifted, 0.0)
            span *= 2
        cnt = jnp.minimum(row + 1, w).astype(F32)
        z = (s / cnt - ug).astype(BF16)
        y = jnp.dot(z, w_ref[g], preferred_element_type=F32) + b_ref[:, cols]
        o_ref[:, cols] = (y * sc_ref[:, cols]).astype(o_ref.dtype)


def _pool(u, w_pool, b_pool, pool_scale):
    B, S, _ = u.shape
    blk = pl.BlockSpec((None, S, D_POOL), lambda b: (b, 0, 0))
    return pl.pallas_call(
        _pool_kernel,
        grid=(B,),
        in_specs=[blk,
                  pl.BlockSpec((len(POOL_WINDOWS), POOL_GROUP_DIM, POOL_GROUP_DIM), lambda b: (0, 0, 0)),
                  pl.BlockSpec((1, D_POOL), lambda b: (0, 0)),
                  pl.BlockSpec((1, D_POOL), lambda b: (0, 0))],
        out_specs=blk,
        out_shape=jax.ShapeDtypeStruct((B, S, D_POOL), BF16),
        compiler_params=_params(("arbitrary",)),
        name="pool_mixer",
    )(u, w_pool, b_pool, pool_scale)


def _outproj_kernel(a_ref, p_ref, x_ref, w_ref, g_ref, rw_ref, rb_ref, x1_ref, h_ref, lg_ref):
    x1 = (x_ref[...]
          + jnp.dot(a_ref[...], w_ref[0:D_ATTN, :], preferred_element_type=F32)
          + jnp.dot(p_ref[...], w_ref[D_ATTN:, :], preferred_element_type=F32))
    x1_ref[...] = x1
    ms = jnp.mean(x1 * x1, axis=-1, keepdims=True)
    h = x1 * lax.rsqrt(ms + RMS_EPS) * g_ref[...]
    _store_token_tiles(h_ref, h)
    lg = jnp.dot(h, rw_ref[...], preferred_element_type=F32,
                 precision=lax.Precision.HIGHEST) + rb_ref[...]
    lg_ref[...] = lg.T[:N_EXPERTS]


def _outproj(attn, pool, x2, w_out, g2, rw, rb):
    T = x2.shape[0]
    tm = TOKEN_TILE
    row = lambda i: (i, 0)
    fixed = lambda i: (0, 0)
    rw = jnp.pad(rw, ((0, 0), (0, LANES - N_EXPERTS)))
    rb = jnp.pad(rb, ((0, 0), (0, LANES - N_EXPERTS)))
    return pl.pallas_call(
        _outproj_kernel,
        grid=(T // tm,),
        in_specs=[pl.BlockSpec((tm, D_ATTN), row),
                  pl.BlockSpec((tm, D_POOL), row),
                  pl.BlockSpec((tm, D_MODEL), row),
                  pl.BlockSpec((D_MODEL, D_MODEL), fixed),
                  pl.BlockSpec((1, D_MODEL), fixed),
                  pl.BlockSpec((D_MODEL, LANES), fixed),
                  pl.BlockSpec((1, LANES), fixed)],
        out_specs=[pl.BlockSpec((tm, D_MODEL), row),
                   pl.BlockSpec((tm * TILE_ROWS, LANES), row),
                   pl.BlockSpec((None, N_EXPERTS, tm), lambda i: (i, 0, 0))],
        out_shape=[jax.ShapeDtypeStruct((T, D_MODEL), F32),
                   jax.ShapeDtypeStruct((T * TILE_ROWS, LANES), F32),
                   jax.ShapeDtypeStruct((T // tm, N_EXPERTS, tm), F32)],
        compiler_params=_params(("arbitrary",)),
        name="outproj_router",
    )(attn, pool, x2, w_out, g2, rw, rb)


def _route_kernel(lt_ref, pos_ref, gate_ref, cnt_ref, idx_s, rank_s):
    nt, E, W = lt_ref.shape
    e_iota = lax.broadcasted_iota(jnp.int32, (E, W), 0)
    before = (lax.broadcasted_iota(jnp.int32, (W, W), 0)
              < lax.broadcasted_iota(jnp.int32, (W, W), 1)).astype(BF16)
    ones = jnp.ones((W, LANES), BF16)
    widen = lambda a: jnp.concatenate([a] * (W // LANES), axis=1)

    def phase1(i, running):
        v = lt_ref[i]
        sel = jnp.zeros((E, W), F32)
        tops, hots = [], []
        for k in range(TOP_K):
            m = jnp.max(v, axis=0, keepdims=True)
            idx = jnp.min(jnp.where(v == m, e_iota, E), axis=0, keepdims=True)
            hot = e_iota == idx
            v = jnp.where(hot, -jnp.inf, v)
            sel = sel + hot.astype(F32)
            idx_s[i, k:k + 1, :] = idx
            tops.append(m)
            hots.append(hot)
        selb = sel.astype(BF16)
        rank = jnp.dot(selb, before, preferred_element_type=F32) + widen(running)
        for k in range(TOP_K):
            rank_s[i, k:k + 1, :] = jnp.sum(jnp.where(hots[k], rank, 0.0), axis=0, keepdims=True)
        ex = [jnp.exp(t - tops[0]) for t in tops]
        den = ex[0] + ex[1] + ex[2] + ex[3]
        for k in range(TOP_K):
            gate_ref[i, k:k + 1, :] = ex[k] / den
        return running + jnp.dot(selb, ones, preferred_element_type=F32)

    counts = lax.fori_loop(0, nt, phase1, jnp.zeros((E, LANES), F32))
    cnt_ref[...] = counts.astype(jnp.int32)

    row = lax.broadcasted_iota(jnp.int32, (E, LANES), 0)
    incl = counts
    span = 1
    while span < E:
        incl = incl + jnp.where(row >= span, pltpu.roll(incl, shift=span, axis=0), 0.0)
        span *= 2
    start = widen(incl - counts)

    def phase2(i, c):
        for k in range(TOP_K):
            hot = e_iota == idx_s[i, k:k + 1, :]
            base = jnp.sum(jnp.where(hot, start, 0.0), axis=0, keepdims=True)
            pos_ref[i, k:k + 1, :] = (base + rank_s[i, k:k + 1, :]).astype(jnp.int32)
        return c

    lax.fori_loop(0, nt, phase2, 0)


def _route(logits_t):
    nt, E, W = logits_t.shape
    whole = lambda shape: pl.BlockSpec(shape, lambda: (0,) * len(shape))
    return pl.pallas_call(
        _route_kernel,
        in_specs=[whole((nt, E, W))],
        out_specs=[whole((nt, TOP_K, W)), whole((nt, TOP_K, W)), whole((E, LANES))],
        out_shape=[jax.ShapeDtypeStruct((nt, TOP_K, W), jnp.int32),
                   jax.ShapeDtypeStruct((nt, TOP_K, W), F32),
                   jax.ShapeDtypeStruct((E, LANES), jnp.int32)],
        scratch_shapes=[pltpu.VMEM((nt, TOP_K, W), jnp.int32),
                        pltpu.VMEM((nt, TOP_K, W), F32)],
        compiler_params=pltpu.CompilerParams(vmem_limit_bytes=VMEM_LIMIT),
        name="route",
    )(logits_t)


def _work_items(counts):
    tm = MOE_TILE
    ends = jnp.cumsum(counts)
    starts = ends - counts
    first_blk = starts // tm
    last_blk = jnp.where(counts > 0, (ends - 1) // tm, first_blk - 1)
    n_items = last_blk - first_blk + 1
    item_end = jnp.cumsum(n_items)
    item_start = item_end - n_items
    return starts, ends, first_blk, item_start, item_end


def _item_table(counts, n_blocks):
    tm = MOE_TILE
    starts, ends, first_blk, item_start, item_end = _work_items(counts)
    n_slots = n_blocks + N_EXPERTS - 1
    j = jnp.arange(n_slots, dtype=jnp.int32)
    e = jnp.minimum(jnp.searchsorted(item_end, j, side='right'), N_EXPERTS - 1).astype(jnp.int32)
    real = j < item_end[-1]
    blk = jnp.where(real, first_blk[e] + j - item_start[e], n_blocks - 1)
    lo = jnp.where(real, jnp.maximum(starts[e], blk * tm) - blk * tm, 0)
    hi = jnp.where(real, jnp.minimum(ends[e], (blk + 1) * tm) - blk * tm, 0)
    last_e = e[jnp.maximum(item_end[-1] - 1, 0)]
    e = jnp.where(real, e, last_e)
    as_i32 = lambda a: a.astype(jnp.int32)
    return as_i32(blk), as_i32(e), as_i32(lo), as_i32(hi)


def _scatter_kernel(pos_ref, src_ref, dst_ref, sem):
    tg = src_ref.shape[0]
    for r in range(tg):
        for k in range(TOP_K):
            n = r * TOP_K + k
            pltpu.make_async_copy(src_ref.at[r], dst_ref.at[pos_ref[0, 0, n]], sem).start(priority=n % 2)
    for k in range(TOP_K):
        pltpu.make_async_copy(src_ref, dst_ref.at[pl.ds(0, tg)], sem).wait()


def _scatter_rows(src, pos):
    T = src.shape[0]
    tg = GATHER_TILE
    return pl.pallas_call(
        _scatter_kernel,
        grid=(T // tg,),
        in_specs=[pl.BlockSpec((1, 1, tg * TOP_K), lambda i: (i, 0, 0), memory_space=pltpu.SMEM),
                  pl.BlockSpec((tg, TILE_ROWS, LANES), lambda i: (i, 0, 0))],
        out_specs=pl.BlockSpec(memory_space=pl.ANY),
        out_shape=jax.ShapeDtypeStruct((T * TOP_K, TILE_ROWS, LANES), src.dtype),
        scratch_shapes=[pltpu.SemaphoreType.DMA(())],
        compiler_params=_params(("arbitrary",)),
        name="row_scatter",
    )(pos.reshape(T // tg, 1, tg * TOP_K), src)


def _moe_kernel(blk_ref, e_ref, lo_ref, hi_ref, xs_ref, wg_ref, bg_ref, wu_ref, bu_ref, wd_ref,
                bd_ref, y_ref, wg_s, wu_s, wd_s):
    j = pl.program_id(0)
    jp = jnp.maximum(j - 1, 0)
    tm = MOE_TILE

    @pl.when(jnp.logical_or(j == 0, e_ref[j] != e_ref[jp]))
    def _():
        rows = 128
        for c in range(D_MODEL // rows):
            sl = slice(c * rows, (c + 1) * rows)
            wg_s[sl, :] = wg_ref[sl, :].astype(BF16)
            wu_s[sl, :] = wu_ref[sl, :].astype(BF16)
            wd_s[sl, :] = wd_ref[sl, :].astype(BF16)

    lo = lo_ref[j]
    hi = hi_ref[j]

    @pl.when(lo < hi)
    def _():
        x = _load_token_tiles(xs_ref, tm, BF16)
        y = None
        for n in range(D_FF // FF_CHUNK):
            cols = slice(n * FF_CHUNK, (n + 1) * FF_CHUNK)
            g = jnp.dot(x, wg_s[:, cols], preferred_element_type=F32) + bg_ref[:, cols]
            g = jnp.minimum(g, SWIGLU_LIMIT)
            u = jnp.dot(x, wu_s[:, cols], preferred_element_type=F32) + bu_ref[:, cols]
            u = jnp.clip(u, -SWIGLU_LIMIT, SWIGLU_LIMIT)
            act = (g * jax.nn.sigmoid(SWIGLU_ALPHA * g) * (u + 1.0)).astype(BF16)
            part = jnp.dot(act, wd_s[cols, :], preferred_element_type=F32)
            y = part if y is None else y + part
        y = y + bd_ref[...]
        rows = lax.broadcasted_iota(jnp.int32, (tm, LANES), 0)
        mine = jnp.logical_and(rows >= lo, rows < hi)
        first_visit = jnp.logical_or(j == 0, blk_ref[j] != blk_ref[jp])

        @pl.when(first_visit)
        def _():
            for c in range(TILE_ROWS):
                y_ref[pl.ds(c, tm, stride=TILE_ROWS), :] = jnp.where(
                    mine, y[:, c * LANES:(c + 1) * LANES], 0.0)

        @pl.when(jnp.logical_not(first_visit))
        def _():
            for c in range(TILE_ROWS):
                sl = pl.ds(c, tm, stride=TILE_ROWS)
                y_ref[sl, :] = jnp.where(mine, y[:, c * LANES:(c + 1) * LANES], y_ref[sl, :])


def _moe_experts(items, xs, wg, bg, wu, bu, wd, bd):
    P = xs.shape[0] // TILE_ROWS
    tm = MOE_TILE
    n_items = items[0].shape[0]
    tiles = pl.BlockSpec((tm * TILE_ROWS, LANES), lambda j, blk, e, lo, hi: (blk[j], 0))
    wmap = lambda j, blk, e, lo, hi: (e[j], 0, 0)
    wspec = pl.BlockSpec((None, D_MODEL, D_FF), wmap)
    bspec = pl.BlockSpec((None, 1, D_FF), wmap)
    grid_spec = pltpu.PrefetchScalarGridSpec(
        num_scalar_prefetch=4,
        grid=(n_items,),
        in_specs=[tiles, wspec, bspec, wspec, bspec, wspec, bspec],
        out_specs=tiles,
        scratch_shapes=[pltpu.VMEM((D_MODEL, D_FF), BF16),
                        pltpu.VMEM((D_MODEL, D_FF), BF16),
                        pltpu.VMEM((D_FF, D_MODEL), BF16)],
    )
    return pl.pallas_call(
        _moe_kernel,
        grid_spec=grid_spec,
        out_shape=jax.ShapeDtypeStruct((P * TILE_ROWS, LANES), F32),
        compiler_params=_params(("arbitrary",)),
        name="moe_experts",
    )(*items, xs, wg, bg, wu, bu, wd, bd)


def _combine_kernel(pos_ref, x1_ref, gt_ref, g_ref, ys_ref, o_ref, buf, sem):
    i = pl.program_id(0)
    n_tiles = pl.num_programs(0) - 1
    tm = x1_ref.shape[0]
    n_rows = TOP_K * tm

    @pl.when(i < n_tiles)
    def _():
        slot = i % 2
        for n in range(n_rows):
            src = ys_ref.at[pl.ds(pl.multiple_of(pos_ref[0, 0, n] * TILE_ROWS, TILE_ROWS), TILE_ROWS), :]
            pltpu.make_async_copy(src, buf.at[slot, pl.ds(n * TILE_ROWS, TILE_ROWS), :],
                                  sem.at[slot]).start(priority=n % 2)

    @pl.when(i > 0)
    def _():
        slot = (i - 1) % 2
        pltpu.make_async_copy(ys_ref.at[pl.ds(0, n_rows * TILE_ROWS), :], buf.at[slot], sem.at[slot]).wait()
        x = x1_ref[...]
        gates = gt_ref[...]
        for k in range(TOP_K):
            x = x + gates[:, k:k + 1] * _load_token_tiles(
                buf.at[slot, pl.ds(k * tm * TILE_ROWS, tm * TILE_ROWS), :], tm)
        ms = jnp.mean(x * x, axis=-1, keepdims=True)
        o_ref[...] = x * lax.rsqrt(ms + RMS_EPS) * g_ref[...]


def _combine(x1, ys, pos_tiles, gates, g):
    T = x1.shape[0]
    tm = COMBINE_TILE
    n_tiles = T // tm
    done = lambda i: (jnp.maximum(i - 1, 0), 0)
    return pl.pallas_call(
        _combine_kernel,
        grid=(n_tiles + 1,),
        in_specs=[pl.BlockSpec((1, 1, TOP_K * tm), lambda i: (jnp.minimum(i, n_tiles - 1), 0, 0),
                               memory_space=pltpu.SMEM),
                  pl.BlockSpec((tm, D_MODEL), done),
                  pl.BlockSpec((tm, TOP_K), done),
                  pl.BlockSpec((1, D_MODEL), lambda i: (0, 0)),
                  pl.BlockSpec(memory_space=pl.ANY)],
        out_specs=pl.BlockSpec((tm, D_MODEL), done),
        out_shape=jax.ShapeDtypeStruct((T, D_MODEL), F32),
        scratch_shapes=[pltpu.VMEM((2, TOP_K * tm * TILE_ROWS, LANES), F32),
                        pltpu.SemaphoreType.DMA((2,))],
        compiler_params=_params(("arbitrary",)),
        name="gather_combine_norm",
    )(pos_tiles, x1, gates, g, ys)


def kernel(x, norm1_g, w_in, lambda_q1, lambda_k1, lambda_q2, lambda_k2, subln_g, w_pool, b_pool,
           pool_scale, w_out, norm2_g, router_w, router_b, w_gate, b_gate, w_up, b_up, w_down,
           b_down, final_g):
    B, S, D = x.shape
    T = B * S
    l = 0
    lambda_init = 0.8 - 0.6 * math.exp(-0.3 * l)
    x2 = x.reshape(T, D)

    w = w_in[l]
    wqv_t = jnp.concatenate([w[:, :D_ATTN], w[:, 2 * D_ATTN:3 * D_ATTN]], axis=1).T.astype(BF16)
    qt, k, vt, u = _inproj(x2, norm1_g[l][None, :], wqv_t, w[:, D_ATTN:2 * D_ATTN].astype(BF16),
                           w[:, 3 * D_ATTN:].astype(BF16), B, S)
    attn = _attention(qt, k.reshape(B, S, D_ATTN), vt,
                      lambda_q1[l][None, :], lambda_k1[l][None, :],
                      lambda_q2[l][None, :], lambda_k2[l][None, :],
                      subln_g[l][None, :], lambda_init)
    pool = _pool(u.reshape(B, S, D_POOL), w_pool[l].astype(BF16),
                 b_pool[l].reshape(1, D_POOL), pool_scale[l][None, :])
    x1, h2, logits_t = _outproj(attn.reshape(T, D_ATTN), pool.reshape(T, D_POOL), x2,
                                w_out[l].astype(BF16), norm2_g[l][None, :],
                                router_w[l], router_b[l][None, :])

    pos_t, gates_t, counts = _route(logits_t)
    pos_tok = pos_t.transpose(0, 2, 1).reshape(T * TOP_K)
    nt, _, W = pos_t.shape
    pos_tiles = (pos_t.reshape(nt, TOP_K, W // COMBINE_TILE, COMBINE_TILE).transpose(0, 2, 1, 3)
                 .reshape(T // COMBINE_TILE, 1, TOP_K * COMBINE_TILE))
    gates = gates_t.transpose(0, 2, 1).reshape(T, TOP_K)
    assert (T * TOP_K) % MOE_TILE == 0
    items = _item_table(counts[:, 0], T * TOP_K // MOE_TILE)

    xs = _scatter_rows(h2.reshape(T, TILE_ROWS, LANES), pos_tok)
    ys = _moe_experts(items, xs.reshape(-1, LANES),
                      w_gate[l], b_gate[l][:, None, :], w_up[l], b_up[l][:, None, :],
                      w_down[l], b_down[l][:, None, :])
    out = _combine(x1, ys, pos_tiles, gates, final_g[None, :])
    return out.reshape(B, S, D)
```

```python
import functools
import math

import jax
import jax.numpy as jnp
from jax import lax
from jax.experimental import pallas as pl
from jax.experimental.pallas import tpu as pltpu

D_MODEL = 1024
D_ATTN = 512
D_POOL = 512
N_DIFF_HEADS = 4
DIFF_QK_DIM = 64
DIFF_V_DIM = 128
POOL_WINDOWS = (2, 4, 8, 16)
POOL_GROUP_DIM = 128
N_EXPERTS = 32
TOP_K = 4
D_FF = 1024
SWIGLU_LIMIT = 7.0
SWIGLU_ALPHA = 1.702
RMS_EPS = 1e-5

F32 = jnp.float32
BF16 = jnp.bfloat16

TOKEN_TILE = 512
ATTN_TILE = 256
MOE_TILE = 512
GATHER_TILE = 256
COMBINE_TILE = 256
FF_CHUNK = 512
VMEM_LIMIT = 56 * 1024 * 1024


def _params(sem, vmem=VMEM_LIMIT):
    return pltpu.CompilerParams(dimension_semantics=sem, vmem_limit_bytes=vmem)


LANES = 128
TILE_ROWS = D_MODEL // LANES
assert TILE_ROWS == 8


def _store_token_tiles(ref, x):
    n = x.shape[0]
    for c in range(TILE_ROWS):
        ref[pl.ds(c, n, stride=TILE_ROWS), :] = x[:, c * LANES:(c + 1) * LANES]


def _load_token_tiles(ref, n, dtype=None):
    cols = [ref[pl.ds(c, n, stride=TILE_ROWS), :] for c in range(TILE_ROWS)]
    if dtype is not None:
        cols = [c.astype(dtype) for c in cols]
    return jnp.concatenate(cols, axis=1)


def _inproj_kernel(x_ref, g_ref, wqv_ref, wk_ref, wu_ref, qt_ref, k_ref, vt_ref, u_ref):
    x = x_ref[...]
    ms = jnp.mean(x * x, axis=-1, keepdims=True)
    h = (x * lax.rsqrt(ms + RMS_EPS) * g_ref[...]).astype(BF16)
    qvt = lax.dot_general(wqv_ref[...], h, (((1,), (1,)), ((), ())), preferred_element_type=F32)
    qt_ref[...] = (qvt[:D_ATTN] * (DIFF_QK_DIM ** -0.5 * math.log2(math.e))).astype(BF16)
    vt_ref[...] = qvt[D_ATTN:].astype(BF16)
    k_ref[...] = jnp.dot(h, wk_ref[...], preferred_element_type=F32).astype(BF16)
    u_ref[...] = jnp.dot(h, wu_ref[...], preferred_element_type=F32)


def _inproj(x2, g, wqv_t, wk, wu, B, S):
    T = x2.shape[0]
    tm = TOKEN_TILE
    per_seq = S // tm
    row = lambda i: (i, 0)
    fixed = lambda i: (0, 0)
    tmap = lambda i: (i // per_seq, 0, i % per_seq)
    return pl.pallas_call(
        _inproj_kernel,
        grid=(T // tm,),
        in_specs=[pl.BlockSpec((tm, D_MODEL), row),
                  pl.BlockSpec((1, D_MODEL), fixed),
                  pl.BlockSpec((2 * D_ATTN, D_MODEL), fixed),
                  pl.BlockSpec((D_MODEL, D_ATTN), fixed),
                  pl.BlockSpec((D_MODEL, D_POOL), fixed)],
        out_specs=[pl.BlockSpec((None, D_ATTN, tm), tmap),
                   pl.BlockSpec((tm, D_ATTN), row),
                   pl.BlockSpec((None, D_ATTN, tm), tmap),
                   pl.BlockSpec((tm, D_POOL), row)],
        out_shape=[jax.ShapeDtypeStruct((B, D_ATTN, S), BF16),
                   jax.ShapeDtypeStruct((T, D_ATTN), BF16),
                   jax.ShapeDtypeStruct((B, D_ATTN, S), BF16),
                   jax.ShapeDtypeStruct((T, D_POOL), F32)],
        compiler_params=_params(("arbitrary",)),
        name="inproj",
    )(x2, g, wqv_t, wk, wu)


def _attn_kernel(qa_ref, qb_ref, k_ref, vt_ref, lq1_ref, lk1_ref, lq2_ref, lk2_ref, sg_ref,
                 o_ref, qs_s, vt_s, m_s, acc_s, s_s, *, lambda_init, n_q):
    tq = qa_ref.shape[1]
    tk = tq
    dv = DIFF_V_DIM
    p = pl.program_id(2)

    @pl.when(p == 0)
    def _():
        for c in range(n_q):
            vt_s[c, 0:dv, :] = vt_ref[:, c * tk:(c + 1) * tk]
            vt_s[c, dv:, :] = jnp.ones((vt_s.shape[1] - dv, tk), BF16)

    feat = lax.broadcasted_iota(jnp.int32, (dv, tq), 0)
    for blk, q_ref in enumerate((qa_ref, qb_ref)):
        qt = q_ref[...]
        zero = jnp.zeros_like(qt)
        qs_s[blk] = jnp.concatenate([jnp.where(feat < DIFF_QK_DIM, qt, zero),
                                     jnp.where(feat >= DIFF_QK_DIM, qt, zero)], axis=1)
    m_s[...] = jnp.full(m_s.shape, -jnp.inf, F32)
    acc_s[...] = jnp.zeros(acc_s.shape, F32)

    key = lax.broadcasted_iota(jnp.int32, (tk, tq), 0)
    qry = lax.broadcasted_iota(jnp.int32, (tk, tq), 1)
    bias = jnp.where(key <= qry, 0.0, -jnp.inf).astype(F32)
    bias = jnp.concatenate([bias, bias], axis=1)

    items = [(0, p, True), (1, n_q - 1 - p, True)]
    for n in range(n_q - 1):
        in_a = n < p
        items.append((jnp.where(in_a, 0, 1), jnp.where(in_a, n, n - p), False))

    def scores(blk, chunk, diag):
        kc = k_ref[pl.ds(pl.multiple_of(chunk * tk, tk), tk), :]
        s = jnp.dot(kc, qs_s[blk], preferred_element_type=F32)
        return s + bias if diag else s

    for t, (blk, chunk, diag) in enumerate(items):
        s = scores(blk, chunk, diag)
        s_s[t] = s
        m_s[blk] = jnp.maximum(m_s[blk], jnp.max(s, axis=0, keepdims=True))

    for t, (blk, chunk, diag) in enumerate(items):
        pt = jnp.exp2(s_s[t] - m_s[blk]).astype(BF16)
        acc_s[blk] += jnp.dot(vt_s[chunk], pt, preferred_element_type=F32)

    lam = (jnp.exp(jnp.sum(lq1_ref[...] * lk1_ref[...]))
           - jnp.exp(jnp.sum(lq2_ref[...] * lk2_ref[...])) + lambda_init)
    for blk, qblock in enumerate((p, n_q - 1 - p)):
        acc = acc_s[blk]
        o0 = acc[0:dv, 0:tq] / acc[dv:dv + 1, 0:tq]
        o1 = acc[0:dv, tq:] / acc[dv:dv + 1, tq:]
        a = o0 - lam * o1
        ms = jnp.mean(a * a, axis=0, keepdims=True)
        y = (a * lax.rsqrt(ms + RMS_EPS)).T * sg_ref[...]
        o_ref[pl.ds(pl.multiple_of(qblock * tq, tq), tq), :] = (y * (1.0 - lambda_init)).astype(o_ref.dtype)


def _attention(qt, k, vt, lq1, lk1, lq2, lk2, subln_g, lambda_init):
    B, _, S = qt.shape
    tq = ATTN_TILE
    n_q = S // tq
    assert S % tq == 0 and n_q % 2 == 0
    ones_rows = 16
    vec = lambda n: pl.BlockSpec((1, n), lambda b, h, p: (0, 0))
    return pl.pallas_call(
        functools.partial(_attn_kernel, lambda_init=lambda_init, n_q=n_q),
        grid=(B, N_DIFF_HEADS, n_q // 2),
        in_specs=[pl.BlockSpec((None, DIFF_V_DIM, tq), lambda b, h, p: (b, h, p)),
                  pl.BlockSpec((None, DIFF_V_DIM, tq), lambda b, h, p: (b, h, n_q - 1 - p)),
                  pl.BlockSpec((None, S, DIFF_V_DIM), lambda b, h, p: (b, 0, h)),
                  pl.BlockSpec((None, DIFF_V_DIM, S), lambda b, h, p: (b, h, 0)),
                  vec(DIFF_QK_DIM), vec(DIFF_QK_DIM), vec(DIFF_QK_DIM), vec(DIFF_QK_DIM),
                  vec(DIFF_V_DIM)],
        out_specs=pl.BlockSpec((None, S, DIFF_V_DIM), lambda b, h, p: (b, 0, h)),
        out_shape=jax.ShapeDtypeStruct((B, S, D_ATTN), BF16),
        scratch_shapes=[pltpu.VMEM((2, DIFF_V_DIM, 2 * tq), BF16),
                        pltpu.VMEM((n_q, DIFF_V_DIM + ones_rows, tq), BF16),
                        pltpu.VMEM((2, 1, 2 * tq), F32),
                        pltpu.VMEM((2, DIFF_V_DIM + ones_rows, 2 * tq), F32),
                        pltpu.VMEM((n_q + 1, tq, 2 * tq), F32)],
        compiler_params=_params(("arbitrary", "arbitrary", "arbitrary")),
        name="diff_attn",
    )(qt, qt, k, vt, lq1, lk1, lq2, lk2, subln_g)


def _pool_kernel(u_ref, w_ref, b_ref, sc_ref, o_ref):
    S = u_ref.shape[0]
    row = lax.broadcasted_iota(jnp.int32, (S, POOL_GROUP_DIM), 0)
    for g, w in enumerate(POOL_WINDOWS):
        cols = slice(g * POOL_GROUP_DIM, (g + 1) * POOL_GROUP_DIM)
        ug = u_ref[:, cols]
        s = ug
        span = 1
        while span < w:
            shifted = pltpu.roll(s, shift=span, axis=0)
            s = s + jnp.where(row >= span, shifted, 0.0)
            span *= 2
        cnt = jnp.minimum(row + 1, w).astype(F32)
        z = (s / cnt - ug).astype(BF16)
        y = jnp.dot(z, w_ref[g], preferred_element_type=F32) + b_ref[:, cols]
        o_ref[:, cols] = (y * sc_ref[:, cols]).astype(o_ref.dtype)


def _pool(u, w_pool, b_pool, pool_scale):
    B, S, _ = u.shape
    blk = pl.BlockSpec((None, S, D_POOL), lambda b: (b, 0, 0))
    return pl.pallas_call(
        _pool_kernel,
        grid=(B,),
        in_specs=[blk,
                  pl.BlockSpec((len(POOL_WINDOWS), POOL_GROUP_DIM, POOL_GROUP_DIM), lambda b: (0, 0, 0)),
                  pl.BlockSpec((1, D_POOL), lambda b: (0, 0)),
                  pl.BlockSpec((1, D_POOL), lambda b: (0, 0))],
        out_specs=blk,
        out_shape=jax.ShapeDtypeStruct((B, S, D_POOL), BF16),
        compiler_params=_params(("arbitrary",)),
        name="pool_mixer",
    )(u, w_pool, b_pool, pool_scale)


def _outproj_kernel(a_ref, p_ref, x_ref, w_ref, g_ref, rw_ref, rb_ref, x1_ref, h_ref, lg_ref):
    x1 = (x_ref[...]
          + jnp.dot(a_ref[...], w_ref[0:D_ATTN, :], preferred_element_type=F32)
          + jnp.dot(p_ref[...], w_ref[D_ATTN:, :], preferred_element_type=F32))
    x1_ref[...] = x1
    ms = jnp.mean(x1 * x1, axis=-1, keepdims=True)
    h = x1 * lax.rsqrt(ms + RMS_EPS) * g_ref[...]
    _store_token_tiles(h_ref, h)
    lg = jnp.dot(h, rw_ref[...], preferred_element_type=F32,
                 precision=lax.Precision.HIGHEST) + rb_ref[...]
    lg_ref[...] = lg.T[:N_EXPERTS]


def _outproj(attn, pool, x2, w_out, g2, rw, rb):
    T = x2.shape[0]
    tm = TOKEN_TILE
    row = lambda i: (i, 0)
    fixed = lambda i: (0, 0)
    rw = jnp.pad(rw, ((0, 0), (0, LANES - N_EXPERTS)))
    rb = jnp.pad(rb, ((0, 0), (0, LANES - N_EXPERTS)))
    return pl.pallas_call(
        _outproj_kernel,
        grid=(T // tm,),
        in_specs=[pl.BlockSpec((tm, D_ATTN), row),
                  pl.BlockSpec((tm, D_POOL), row),
                  pl.BlockSpec((tm, D_MODEL), row),
                  pl.BlockSpec((D_MODEL, D_MODEL), fixed),
                  pl.BlockSpec((1, D_MODEL), fixed),
                  pl.BlockSpec((D_MODEL, LANES), fixed),
                  pl.BlockSpec((1, LANES), fixed)],
        out_specs=[pl.BlockSpec((tm, D_MODEL), row),
                   pl.BlockSpec((tm * TILE_ROWS, LANES), row),
                   pl.BlockSpec((None, N_EXPERTS, tm), lambda i: (i, 0, 0))],
        out_shape=[jax.ShapeDtypeStruct((T, D_MODEL), F32),
                   jax.ShapeDtypeStruct((T * TILE_ROWS, LANES), F32),
                   jax.ShapeDtypeStruct((T // tm, N_EXPERTS, tm), F32)],
        compiler_params=_params(("arbitrary",)),
        name="outproj_router",
    )(attn, pool, x2, w_out, g2, rw, rb)


def _route_kernel(lt_ref, pos_ref, gate_ref, cnt_ref, idx_s, rank_s):
    nt, E, W = lt_ref.shape
    e_iota = lax.broadcasted_iota(jnp.int32, (E, W), 0)
    before = (lax.broadcasted_iota(jnp.int32, (W, W), 0)
              < lax.broadcasted_iota(jnp.int32, (W, W), 1)).astype(BF16)
    ones = jnp.ones((W, LANES), BF16)
    widen = lambda a: jnp.concatenate([a] * (W // LANES), axis=1)

    def phase1(i, running):
        v = lt_ref[i]
        sel = jnp.zeros((E, W), F32)
        tops, hots = [], []
        for k in range(TOP_K):
            m = jnp.max(v, axis=0, keepdims=True)
            idx = jnp.min(jnp.where(v == m, e_iota, E), axis=0, keepdims=True)
            hot = e_iota == idx
            v = jnp.where(hot, -jnp.inf, v)
            sel = sel + hot.astype(F32)
            idx_s[i, k:k + 1, :] = idx
            tops.append(m)
            hots.append(hot)
        selb = sel.astype(BF16)
        rank = jnp.dot(selb, before, preferred_element_type=F32) + widen(running)
        for k in range(TOP_K):
            rank_s[i, k:k + 1, :] = jnp.sum(jnp.where(hots[k], rank, 0.0), axis=0, keepdims=True)
        ex = [jnp.exp(t - tops[0]) for t in tops]
        den = ex[0] + ex[1] + ex[2] + ex[3]
        for k in range(TOP_K):
            gate_ref[i, k:k + 1, :] = ex[k] / den
        return running + jnp.dot(selb, ones, preferred_element_type=F32)

    counts = lax.fori_loop(0, nt, phase1, jnp.zeros((E, LANES), F32))
    cnt_ref[...] = counts.astype(jnp.int32)

    row = lax.broadcasted_iota(jnp.int32, (E, LANES), 0)
    incl = counts
    span = 1
    while span < E:
        incl = incl + jnp.where(row >= span, pltpu.roll(incl, shift=span, axis=0), 0.0)
        span *= 2
    start = widen(incl - counts)

    def phase2(i, c):
        for k in range(TOP_K):
            hot = e_iota == idx_s[i, k:k + 1, :]
            base = jnp.sum(jnp.where(hot, start, 0.0), axis=0, keepdims=True)
            pos_ref[i, k:k + 1, :] = (base + rank_s[i, k:k + 1, :]).astype(jnp.int32)
        return c

    lax.fori_loop(0, nt, phase2, 0)


def _route(logits_t):
    nt, E, W = logits_t.shape
    whole = lambda shape: pl.BlockSpec(shape, lambda: (0,) * len(shape))
    return pl.pallas_call(
        _route_kernel,
        in_specs=[whole((nt, E, W))],
        out_specs=[whole((nt, TOP_K, W)), whole((nt, TOP_K, W)), whole((E, LANES))],
        out_shape=[jax.ShapeDtypeStruct((nt, TOP_K, W), jnp.int32),
                   jax.ShapeDtypeStruct((nt, TOP_K, W), F32),
                   jax.ShapeDtypeStruct((E, LANES), jnp.int32)],
        scratch_shapes=[pltpu.VMEM((nt, TOP_K, W), jnp.int32),
                        pltpu.VMEM((nt, TOP_K, W), F32)],
        compiler_params=pltpu.CompilerParams(vmem_limit_bytes=VMEM_LIMIT),
        name="route",
    )(logits_t)


def _work_items(counts):
    tm = MOE_TILE
    ends = jnp.cumsum(counts)
    starts = ends - counts
    first_blk = starts // tm
    last_blk = jnp.where(counts > 0, (ends - 1) // tm, first_blk - 1)
    n_items = last_blk - first_blk + 1
    item_end = jnp.cumsum(n_items)
    item_start = item_end - n_items
    return starts, ends, first_blk, item_start, item_end


def _item_table(counts, n_blocks):
    tm = MOE_TILE
    starts, ends, first_blk, item_start, item_end = _work_items(counts)
    n_slots = n_blocks + N_EXPERTS - 1
    j = jnp.arange(n_slots, dtype=jnp.int32)
    e = jnp.minimum(jnp.searchsorted(item_end, j, side='right'), N_EXPERTS - 1).astype(jnp.int32)
    real = j < item_end[-1]
    blk = jnp.where(real, first_blk[e] + j - item_start[e], n_blocks - 1)
    lo = jnp.where(real, jnp.maximum(starts[e], blk * tm) - blk * tm, 0)
    hi = jnp.where(real, jnp.minimum(ends[e], (blk + 1) * tm) - blk * tm, 0)
    last_e = e[jnp.maximum(item_end[-1] - 1, 0)]
    e = jnp.where(real, e, last_e)
    as_i32 = lambda a: a.astype(jnp.int32)
    return as_i32(blk), as_i32(e), as_i32(lo), as_i32(hi)


def _scatter_kernel(pos_ref, src_ref, dst_ref, sem):
    tg = src_ref.shape[0]
    for r in range(tg):
        for k in range(TOP_K):
            n = r * TOP_K + k
            pltpu.make_async_copy(src_ref.at[r], dst_ref.at[pos_ref[0, 0, n]], sem).start(priority=n % 2)
    for k in range(TOP_K):
        pltpu.make_async_copy(src_ref, dst_ref.at[pl.ds(0, tg)], sem).wait()


def _scatter_rows(src, pos):
    T = src.shape[0]
    tg = GATHER_TILE
    return pl.pallas_call(
        _scatter_kernel,
        grid=(T // tg,),
        in_specs=[pl.BlockSpec((1, 1, tg * TOP_K), lambda i: (i, 0, 0), memory_space=pltpu.SMEM),
                  pl.BlockSpec((tg, TILE_ROWS, LANES), lambda i: (i, 0, 0))],
        out_specs=pl.BlockSpec(memory_space=pl.ANY),
        out_shape=jax.ShapeDtypeStruct((T * TOP_K, TILE_ROWS, LANES), src.dtype),
        scratch_shapes=[pltpu.SemaphoreType.DMA(())],
        compiler_params=_params(("arbitrary",)),
        name="row_scatter",
    )(pos.reshape(T // tg, 1, tg * TOP_K), src)


def _moe_kernel(blk_ref, e_ref, lo_ref, hi_ref, xs_ref, wg_ref, bg_ref, wu_ref, bu_ref, wd_ref,
                bd_ref, y_ref, wg_s, wu_s, wd_s):
    j = pl.program_id(0)
    jp = jnp.maximum(j - 1, 0)
    tm = MOE_TILE

    @pl.when(jnp.logical_or(j == 0, e_ref[j] != e_ref[jp]))
    def _():
        rows = 128
        for c in range(D_MODEL // rows):
            sl = slice(c * rows, (c + 1) * rows)
            wg_s[sl, :] = wg_ref[sl, :].astype(BF16)
            wu_s[sl, :] = wu_ref[sl, :].astype(BF16)
            wd_s[sl, :] = wd_ref[sl, :].astype(BF16)

    lo = lo_ref[j]
    hi = hi_ref[j]

    first_visit = jnp.logical_or(j == 0, blk_ref[j] != blk_ref[jp])
    half = tm // 2

    def tile_rows(r0, n, c):
        return pl.ds(r0 * TILE_ROWS + c, n, stride=TILE_ROWS)

    def mlp(r0, n):
        x = jnp.concatenate([xs_ref[tile_rows(r0, n, c), :].astype(BF16) for c in range(TILE_ROWS)],
                            axis=1)
        y = None
        for f in range(D_FF // FF_CHUNK):
            cols = slice(f * FF_CHUNK, (f + 1) * FF_CHUNK)
            g = jnp.dot(x, wg_s[:, cols], preferred_element_type=F32) + bg_ref[:, cols]
            g = jnp.minimum(g, SWIGLU_LIMIT)
            u = jnp.dot(x, wu_s[:, cols], preferred_element_type=F32) + bu_ref[:, cols]
            u = jnp.clip(u, -SWIGLU_LIMIT, SWIGLU_LIMIT)
            act = (g * jax.nn.sigmoid(SWIGLU_ALPHA * g) * (u + 1.0)).astype(BF16)
            part = jnp.dot(act, wd_s[cols, :], preferred_element_type=F32)
            y = part if y is None else y + part
        return y + bd_ref[...]

    @pl.when(jnp.logical_and(lo == 0, hi == tm))
    def _():
        y = mlp(0, tm)
        for c in range(TILE_ROWS):
            y_ref[tile_rows(0, tm, c), :] = y[:, c * LANES:(c + 1) * LANES]

    def partial_half(r0):
        touches = jnp.logical_and(lo < r0 + half, hi > r0)

        @pl.when(touches)
        def _():
            y = mlp(r0, half)
            rows = r0 + lax.broadcasted_iota(jnp.int32, (half, LANES), 0)
            mine = jnp.logical_and(rows >= lo, rows < hi)

            @pl.when(first_visit)
            def _():
                for c in range(TILE_ROWS):
                    y_ref[tile_rows(r0, half, c), :] = jnp.where(
                        mine, y[:, c * LANES:(c + 1) * LANES], 0.0)

            @pl.when(jnp.logical_not(first_visit))
            def _():
                for c in range(TILE_ROWS):
                    sl = tile_rows(r0, half, c)
                    y_ref[sl, :] = jnp.where(mine, y[:, c * LANES:(c + 1) * LANES], y_ref[sl, :])

        @pl.when(jnp.logical_and(jnp.logical_not(touches), first_visit))
        def _():
            y_ref[pl.ds(r0 * TILE_ROWS, half * TILE_ROWS), :] = jnp.zeros(
                (half * TILE_ROWS, LANES), y_ref.dtype)

    @pl.when(jnp.logical_and(lo < hi, jnp.logical_or(lo > 0, hi < tm)))
    def _():
        partial_half(0)
        partial_half(half)


def _moe_experts(items, xs, wg, bg, wu, bu, wd, bd):
    P = xs.shape[0] // TILE_ROWS
    tm = MOE_TILE
    n_items = items[0].shape[0]
    tiles = pl.BlockSpec((tm * TILE_ROWS, LANES), lambda j, blk, e, lo, hi: (blk[j], 0))
    wmap = lambda j, blk, e, lo, hi: (e[j], 0, 0)
    wspec = pl.BlockSpec((None, D_MODEL, D_FF), wmap)
    bspec = pl.BlockSpec((None, 1, D_FF), wmap)
    grid_spec = pltpu.PrefetchScalarGridSpec(
        num_scalar_prefetch=4,
        grid=(n_items,),
        in_specs=[tiles, wspec, bspec, wspec, bspec, wspec, bspec],
        out_specs=tiles,
        scratch_shapes=[pltpu.VMEM((D_MODEL, D_FF), BF16),
                        pltpu.VMEM((D_MODEL, D_FF), BF16),
                        pltpu.VMEM((D_FF, D_MODEL), BF16)],
    )
    return pl.pallas_call(
        _moe_kernel,
        grid_spec=grid_spec,
        out_shape=jax.ShapeDtypeStruct((P * TILE_ROWS, LANES), F32),
        compiler_params=_params(("arbitrary",)),
        name="moe_experts",
    )(*items, xs, wg, bg, wu, bu, wd, bd)


def _combine_kernel(pos_ref, x1_ref, gt_ref, g_ref, ys_ref, o_ref, buf, sem):
    i = pl.program_id(0)
    n_tiles = pl.num_programs(0) - 1
    tm = x1_ref.shape[0]
    n_rows = TOP_K * tm

    @pl.when(i < n_tiles)
    def _():
        slot = i % 2
        for n in range(n_rows):
            src = ys_ref.at[pl.ds(pl.multiple_of(pos_ref[0, 0, n] * TILE_ROWS, TILE_ROWS), TILE_ROWS), :]
            pltpu.make_async_copy(src, buf.at[slot, pl.ds(n * TILE_ROWS, TILE_ROWS), :],
                                  sem.at[slot]).start(priority=n % 2)

    @pl.when(i > 0)
    def _():
        slot = (i - 1) % 2
        pltpu.make_async_copy(ys_ref.at[pl.ds(0, n_rows * TILE_ROWS), :], buf.at[slot], sem.at[slot]).wait()
        x = x1_ref[...]
        gates = gt_ref[...]
        for k in range(TOP_K):
            x = x + gates[:, k:k + 1] * _load_token_tiles(
                buf.at[slot, pl.ds(k * tm * TILE_ROWS, tm * TILE_ROWS), :], tm)
        ms = jnp.mean(x * x, axis=-1, keepdims=True)
        o_ref[...] = x * lax.rsqrt(ms + RMS_EPS) * g_ref[...]


def _combine(x1, ys, pos_tiles, gates, g):
    T = x1.shape[0]
    tm = COMBINE_TILE
    n_tiles = T // tm
    done = lambda i: (jnp.maximum(i - 1, 0), 0)
    return pl.pallas_call(
        _combine_kernel,
        grid=(n_tiles + 1,),
        in_specs=[pl.BlockSpec((1, 1, TOP_K * tm), lambda i: (jnp.minimum(i, n_tiles - 1), 0, 0),
                               memory_space=pltpu.SMEM),
                  pl.BlockSpec((tm, D_MODEL), done),
                  pl.BlockSpec((tm, TOP_K), done),
                  pl.BlockSpec((1, D_MODEL), lambda i: (0, 0)),
                  pl.BlockSpec(memory_space=pl.ANY)],
        out_specs=pl.BlockSpec((tm, D_MODEL), done),
        out_shape=jax.ShapeDtypeStruct((T, D_MODEL), F32),
        scratch_shapes=[pltpu.VMEM((2, TOP_K * tm * TILE_ROWS, LANES), F32),
                        pltpu.SemaphoreType.DMA((2,))],
        compiler_params=_params(("arbitrary",)),
        name="gather_combine_norm",
    )(pos_tiles, x1, gates, g, ys)


def kernel(x, norm1_g, w_in, lambda_q1, lambda_k1, lambda_q2, lambda_k2, subln_g, w_pool, b_pool,
           pool_scale, w_out, norm2_g, router_w, router_b, w_gate, b_gate, w_up, b_up, w_down,
           b_down, final_g):
    B, S, D = x.shape
    T = B * S
    l = 0
    lambda_init = 0.8 - 0.6 * math.exp(-0.3 * l)
    x2 = x.reshape(T, D)

    w = w_in[l]
    wqv_t = jnp.concatenate([w[:, :D_ATTN], w[:, 2 * D_ATTN:3 * D_ATTN]], axis=1).T.astype(BF16)
    qt, k, vt, u = _inproj(x2, norm1_g[l][None, :], wqv_t, w[:, D_ATTN:2 * D_ATTN].astype(BF16),
                           w[:, 3 * D_ATTN:].astype(BF16), B, S)
    attn = _attention(qt, k.reshape(B, S, D_ATTN), vt,
                      lambda_q1[l][None, :], lambda_k1[l][None, :],
                      lambda_q2[l][None, :], lambda_k2[l][None, :],
                      subln_g[l][None, :], lambda_init)
    pool = _pool(u.reshape(B, S, D_POOL), w_pool[l].astype(BF16),
                 b_pool[l].reshape(1, D_POOL), pool_scale[l][None, :])
    x1, h2, logits_t = _outproj(attn.reshape(T, D_ATTN), pool.reshape(T, D_POOL), x2,
                                w_out[l].astype(BF16), norm2_g[l][None, :],
                                router_w[l], router_b[l][None, :])

    pos_t, gates_t, counts = _route(logits_t)
    pos_tok = pos_t.transpose(0, 2, 1).reshape(T * TOP_K)
    nt, _, W = pos_t.shape
    pos_tiles = (pos_t.reshape(nt, TOP_K, W // COMBINE_TILE, COMBINE_TILE).transpose(0, 2, 1, 3)
                 .reshape(T // COMBINE_TILE, 1, TOP_K * COMBINE_TILE))
    gates = gates_t.transpose(0, 2, 1).reshape(T, TOP_K)
    assert (T * TOP_K) % MOE_TILE == 0
    items = _item_table(counts[:, 0], T * TOP_K // MOE_TILE)

    xs = _scatter_rows(h2.reshape(T, TILE_ROWS, LANES), pos_tok)
    ys = _moe_experts(items, xs.reshape(-1, LANES),
                      w_gate[l], b_gate[l][:, None, :], w_up[l], b_up[l][:, None, :],
                      w_down[l], b_down[l][:, None, :])
    out = _combine(x1, ys, pos_tiles, gates, final_g[None, :])
    return out.reshape(B, S, D)
```

```python
import functools
import math

import jax
import jax.numpy as jnp
from jax import lax
from jax.experimental import pallas as pl
from jax.experimental.pallas import tpu as pltpu

D_MODEL = 1024
D_ATTN = 512
D_POOL = 512
N_DIFF_HEADS = 4
DIFF_QK_DIM = 64
DIFF_V_DIM = 128
POOL_WINDOWS = (2, 4, 8, 16)
POOL_GROUP_DIM = 128
N_EXPERTS = 32
TOP_K = 4
D_FF = 1024
SWIGLU_LIMIT = 7.0
SWIGLU_ALPHA = 1.702
RMS_EPS = 1e-5

F32 = jnp.float32
BF16 = jnp.bfloat16

TOKEN_TILE = 512
ATTN_TILE = 256
MOE_TILE = 512
GATHER_TILE = 256
COMBINE_TILE = 256
FF_CHUNK = 512
VMEM_LIMIT = 56 * 1024 * 1024


def _params(sem, vmem=VMEM_LIMIT):
    return pltpu.CompilerParams(dimension_semantics=sem, vmem_limit_bytes=vmem)


LANES = 128
TILE_ROWS = D_MODEL // LANES
assert TILE_ROWS == 8


def _store_token_tiles(ref, x):
    n = x.shape[0]
    for c in range(TILE_ROWS):
        ref[pl.ds(c, n, stride=TILE_ROWS), :] = x[:, c * LANES:(c + 1) * LANES]


def _load_token_tiles(ref, n, dtype=None):
    cols = [ref[pl.ds(c, n, stride=TILE_ROWS), :] for c in range(TILE_ROWS)]
    if dtype is not None:
        cols = [c.astype(dtype) for c in cols]
    return jnp.concatenate(cols, axis=1)


def _inproj_kernel(x_ref, g_ref, wqv_ref, wk_ref, wu_ref, qt_ref, k_ref, vt_ref, u_ref):
    x = x_ref[...]
    ms = jnp.mean(x * x, axis=-1, keepdims=True)
    h = (x * lax.rsqrt(ms + RMS_EPS) * g_ref[...]).astype(BF16)
    qvt = lax.dot_general(wqv_ref[...], h, (((1,), (1,)), ((), ())), preferred_element_type=F32)
    qt_ref[...] = (qvt[:D_ATTN] * (DIFF_QK_DIM ** -0.5 * math.log2(math.e))).astype(BF16)
    vt_ref[...] = qvt[D_ATTN:].astype(BF16)
    k_ref[...] = jnp.dot(h, wk_ref[...], preferred_element_type=F32).astype(BF16)
    u_ref[...] = jnp.dot(h, wu_ref[...], preferred_element_type=F32)


def _inproj(x2, g, wqv_t, wk, wu, B, S):
    T = x2.shape[0]
    tm = TOKEN_TILE
    per_seq = S // tm
    row = lambda i: (i, 0)
    fixed = lambda i: (0, 0)
    tmap = lambda i: (i // per_seq, 0, i % per_seq)
    return pl.pallas_call(
        _inproj_kernel,
        grid=(T // tm,),
        in_specs=[pl.BlockSpec((tm, D_MODEL), row),
                  pl.BlockSpec((1, D_MODEL), fixed),
                  pl.BlockSpec((2 * D_ATTN, D_MODEL), fixed),
                  pl.BlockSpec((D_MODEL, D_ATTN), fixed),
                  pl.BlockSpec((D_MODEL, D_POOL), fixed)],
        out_specs=[pl.BlockSpec((None, D_ATTN, tm), tmap),
                   pl.BlockSpec((tm, D_ATTN), row),
                   pl.BlockSpec((None, D_ATTN, tm), tmap),
                   pl.BlockSpec((tm, D_POOL), row)],
        out_shape=[jax.ShapeDtypeStruct((B, D_ATTN, S), BF16),
                   jax.ShapeDtypeStruct((T, D_ATTN), BF16),
                   jax.ShapeDtypeStruct((B, D_ATTN, S), BF16),
                   jax.ShapeDtypeStruct((T, D_POOL), F32)],
        compiler_params=_params(("arbitrary",)),
        name="inproj",
    )(x2, g, wqv_t, wk, wu)


def _attn_kernel(qa_ref, qb_ref, k_ref, vt_ref, lq1_ref, lk1_ref, lq2_ref, lk2_ref, sg_ref,
                 o_ref, qs_s, vt_s, m_s, acc_s, s_s, *, lambda_init, n_q):
    tq = qa_ref.shape[1]
    tk = tq
    dv = DIFF_V_DIM
    p = pl.program_id(2)

    @pl.when(p == 0)
    def _():
        for c in range(n_q):
            vt_s[c, 0:dv, :] = vt_ref[:, c * tk:(c + 1) * tk]
            vt_s[c, dv:, :] = jnp.ones((vt_s.shape[1] - dv, tk), BF16)

    feat = lax.broadcasted_iota(jnp.int32, (dv, tq), 0)
    for blk, q_ref in enumerate((qa_ref, qb_ref)):
        qt = q_ref[...]
        zero = jnp.zeros_like(qt)
        qs_s[blk] = jnp.concatenate([jnp.where(feat < DIFF_QK_DIM, qt, zero),
                                     jnp.where(feat >= DIFF_QK_DIM, qt, zero)], axis=1)
    m_s[...] = jnp.full(m_s.shape, -jnp.inf, F32)
    acc_s[...] = jnp.zeros(acc_s.shape, F32)

    key = lax.broadcasted_iota(jnp.int32, (tk, tq), 0)
    qry = lax.broadcasted_iota(jnp.int32, (tk, tq), 1)
    bias = jnp.where(key <= qry, 0.0, -jnp.inf).astype(F32)
    bias = jnp.concatenate([bias, bias], axis=1)

    items = [(0, p, True), (1, n_q - 1 - p, True)]
    for n in range(n_q - 1):
        in_a = n < p
        items.append((jnp.where(in_a, 0, 1), jnp.where(in_a, n, n - p), False))

    def scores(blk, chunk, diag):
        kc = k_ref[pl.ds(pl.multiple_of(chunk * tk, tk), tk), :]
        s = jnp.dot(kc, qs_s[blk], preferred_element_type=F32)
        return s + bias if diag else s

    for t, (blk, chunk, diag) in enumerate(items):
        s = scores(blk, chunk, diag)
        s_s[t] = s
        m_s[blk] = jnp.maximum(m_s[blk], jnp.max(s, axis=0, keepdims=True))

    for t, (blk, chunk, diag) in enumerate(items):
        pt = jnp.exp2(s_s[t] - m_s[blk]).astype(BF16)
        acc_s[blk] += jnp.dot(vt_s[chunk], pt, preferred_element_type=F32)

    lam = (jnp.exp(jnp.sum(lq1_ref[...] * lk1_ref[...]))
           - jnp.exp(jnp.sum(lq2_ref[...] * lk2_ref[...])) + lambda_init)
    for blk, qblock in enumerate((p, n_q - 1 - p)):
        acc = acc_s[blk]
        o0 = acc[0:dv, 0:tq] / acc[dv:dv + 1, 0:tq]
        o1 = acc[0:dv, tq:] / acc[dv:dv + 1, tq:]
        a = o0 - lam * o1
        ms = jnp.mean(a * a, axis=0, keepdims=True)
        y = (a * lax.rsqrt(ms + RMS_EPS)).T * sg_ref[...]
        o_ref[pl.ds(pl.multiple_of(qblock * tq, tq), tq), :] = (y * (1.0 - lambda_init)).astype(o_ref.dtype)


def _attention(qt, k, vt, lq1, lk1, lq2, lk2, subln_g, lambda_init):
    B, _, S = qt.shape
    tq = ATTN_TILE
    n_q = S // tq
    assert S % tq == 0 and n_q % 2 == 0
    ones_rows = 16
    vec = lambda n: pl.BlockSpec((1, n), lambda b, h, p: (0, 0))
    return pl.pallas_call(
        functools.partial(_attn_kernel, lambda_init=lambda_init, n_q=n_q),
        grid=(B, N_DIFF_HEADS, n_q // 2),
        in_specs=[pl.BlockSpec((None, DIFF_V_DIM, tq), lambda b, h, p: (b, h, p)),
                  pl.BlockSpec((None, DIFF_V_DIM, tq), lambda b, h, p: (b, h, n_q - 1 - p)),
                  pl.BlockSpec((None, S, DIFF_V_DIM), lambda b, h, p: (b, 0, h)),
                  pl.BlockSpec((None, DIFF_V_DIM, S), lambda b, h, p: (b, h, 0)),
                  vec(DIFF_QK_DIM), vec(DIFF_QK_DIM), vec(DIFF_QK_DIM), vec(DIFF_QK_DIM),
                  vec(DIFF_V_DIM)],
        out_specs=pl.BlockSpec((None, S, DIFF_V_DIM), lambda b, h, p: (b, 0, h)),
        out_shape=jax.ShapeDtypeStruct((B, S, D_ATTN), BF16),
        scratch_shapes=[pltpu.VMEM((2, DIFF_V_DIM, 2 * tq), BF16),
                        pltpu.VMEM((n_q, DIFF_V_DIM + ones_rows, tq), BF16),
                        pltpu.VMEM((2, 1, 2 * tq), F32),
                        pltpu.VMEM((2, DIFF_V_DIM + ones_rows, 2 * tq), F32),
                        pltpu.VMEM((n_q + 1, tq, 2 * tq), F32)],
        compiler_params=_params(("arbitrary", "arbitrary", "arbitrary")),
        name="diff_attn",
    )(qt, qt, k, vt, lq1, lk1, lq2, lk2, subln_g)


def _pool_kernel(u_ref, w_ref, b_ref, sc_ref, o_ref):
    S = u_ref.shape[0]
    row = lax.broadcasted_iota(jnp.int32, (S, POOL_GROUP_DIM), 0)
    for g, w in enumerate(POOL_WINDOWS):
        cols = slice(g * POOL_GROUP_DIM, (g + 1) * POOL_GROUP_DIM)
        ug = u_ref[:, cols]
        s = ug
        span = 1
        while span < w:
            shifted = pltpu.roll(s, shift=span, axis=0)
            s = s + jnp.where(row >= span, shifted, 0.0)
            span *= 2
        cnt = jnp.minimum(row + 1, w).astype(F32)
        z = (s / cnt - ug).astype(BF16)
        y = jnp.dot(z, w_ref[g], preferred_element_type=F32) + b_ref[:, cols]
        o_ref[:, cols] = (y * sc_ref[:, cols]).astype(o_ref.dtype)


def _pool(u, w_pool, b_pool, pool_scale):
    B, S, _ = u.shape
    blk = pl.BlockSpec((None, S, D_POOL), lambda b: (b, 0, 0))
    return pl.pallas_call(
        _pool_kernel,
        grid=(B,),
        in_specs=[blk,
                  pl.BlockSpec((len(POOL_WINDOWS), POOL_GROUP_DIM, POOL_GROUP_DIM), lambda b: (0, 0, 0)),
                  pl.BlockSpec((1, D_POOL), lambda b: (0, 0)),
                  pl.BlockSpec((1, D_POOL), lambda b: (0, 0))],
        out_specs=blk,
        out_shape=jax.ShapeDtypeStruct((B, S, D_POOL), BF16),
        compiler_params=_params(("arbitrary",)),
        name="pool_mixer",
    )(u, w_pool, b_pool, pool_scale)


def _outproj_kernel(a_ref, p_ref, x_ref, w_ref, g_ref, rw_ref, rb_ref, x1_ref, h_ref, lg_ref):
    x1 = (x_ref[...]
          + jnp.dot(a_ref[...], w_ref[0:D_ATTN, :], preferred_element_type=F32)
          + jnp.dot(p_ref[...], w_ref[D_ATTN:, :], preferred_element_type=F32))
    x1_ref[...] = x1
    ms = jnp.mean(x1 * x1, axis=-1, keepdims=True)
    h = x1 * lax.rsqrt(ms + RMS_EPS) * g_ref[...]
    _store_token_tiles(h_ref, h)
    h_hi = h.astype(BF16)
    h_lo = (h - h_hi.astype(F32)).astype(BF16)
    both = jnp.dot(h_hi, rw_ref[...], preferred_element_type=F32)
    lg = (both[:, :LANES] + both[:, LANES:]
          + jnp.dot(h_lo, rw_ref[:, :LANES], preferred_element_type=F32) + rb_ref[...])
    lg_ref[...] = lg.T[:N_EXPERTS]


def _outproj(attn, pool, x2, w_out, g2, rw, rb):
    T = x2.shape[0]
    tm = TOKEN_TILE
    row = lambda i: (i, 0)
    fixed = lambda i: (0, 0)
    pad = ((0, 0), (0, LANES - N_EXPERTS))
    rw_hi = rw.astype(BF16)
    rw_lo = (rw - rw_hi.astype(F32)).astype(BF16)
    rw = jnp.concatenate([jnp.pad(rw_hi, pad), jnp.pad(rw_lo, pad)], axis=1)
    rb = jnp.pad(rb, pad)
    return pl.pallas_call(
        _outproj_kernel,
        grid=(T // tm,),
        in_specs=[pl.BlockSpec((tm, D_ATTN), row),
                  pl.BlockSpec((tm, D_POOL), row),
                  pl.BlockSpec((tm, D_MODEL), row),
                  pl.BlockSpec((D_MODEL, D_MODEL), fixed),
                  pl.BlockSpec((1, D_MODEL), fixed),
                  pl.BlockSpec((D_MODEL, 2 * LANES), fixed),
                  pl.BlockSpec((1, LANES), fixed)],
        out_specs=[pl.BlockSpec((tm, D_MODEL), row),
                   pl.BlockSpec((tm * TILE_ROWS, LANES), row),
                   pl.BlockSpec((None, N_EXPERTS, tm), lambda i: (i, 0, 0))],
        out_shape=[jax.ShapeDtypeStruct((T, D_MODEL), F32),
                   jax.ShapeDtypeStruct((T * TILE_ROWS, LANES), F32),
                   jax.ShapeDtypeStruct((T // tm, N_EXPERTS, tm), F32)],
        compiler_params=_params(("arbitrary",)),
        name="outproj_router",
    )(attn, pool, x2, w_out, g2, rw, rb)


def _route_kernel(lt_ref, pos_ref, gate_ref, cnt_ref, idx_s, rank_s):
    nt, E, W = lt_ref.shape
    e_iota = lax.broadcasted_iota(jnp.int32, (E, W), 0)
    before = (lax.broadcasted_iota(jnp.int32, (W, W), 0)
              < lax.broadcasted_iota(jnp.int32, (W, W), 1)).astype(BF16)
    ones = jnp.ones((W, LANES), BF16)
    widen = lambda a: jnp.concatenate([a] * (W // LANES), axis=1)

    def phase1(i, running):
        v = lt_ref[i]
        sel = jnp.zeros((E, W), F32)
        tops, hots = [], []
        for k in range(TOP_K):
            m = jnp.max(v, axis=0, keepdims=True)
            idx = jnp.min(jnp.where(v == m, e_iota, E), axis=0, keepdims=True)
            hot = e_iota == idx
            v = jnp.where(hot, -jnp.inf, v)
            sel = sel + hot.astype(F32)
            idx_s[i, k:k + 1, :] = idx
            tops.append(m)
            hots.append(hot)
        selb = sel.astype(BF16)
        rank = jnp.dot(selb, before, preferred_element_type=F32) + widen(running)
        for k in range(TOP_K):
            rank_s[i, k:k + 1, :] = jnp.sum(jnp.where(hots[k], rank, 0.0), axis=0, keepdims=True)
        ex = [jnp.exp(t - tops[0]) for t in tops]
        den = ex[0] + ex[1] + ex[2] + ex[3]
        for k in range(TOP_K):
            gate_ref[i, k:k + 1, :] = ex[k] / den
        return running + jnp.dot(selb, ones, preferred_element_type=F32)

    counts = lax.fori_loop(0, nt, phase1, jnp.zeros((E, LANES), F32))
    cnt_ref[...] = counts.astype(jnp.int32)

    row = lax.broadcasted_iota(jnp.int32, (E, LANES), 0)
    incl = counts
    span = 1
    while span < E:
        incl = incl + jnp.where(row >= span, pltpu.roll(incl, shift=span, axis=0), 0.0)
        span *= 2
    start = widen(incl - counts)

    def phase2(i, c):
        for k in range(TOP_K):
            hot = e_iota == idx_s[i, k:k + 1, :]
            base = jnp.sum(jnp.where(hot, start, 0.0), axis=0, keepdims=True)
            pos_ref[i, k:k + 1, :] = (base + rank_s[i, k:k + 1, :]).astype(jnp.int32)
        return c

    lax.fori_loop(0, nt, phase2, 0)


def _route(logits_t):
    nt, E, W = logits_t.shape
    whole = lambda shape: pl.BlockSpec(shape, lambda: (0,) * len(shape))
    return pl.pallas_call(
        _route_kernel,
        in_specs=[whole((nt, E, W))],
        out_specs=[whole((nt, TOP_K, W)), whole((nt, TOP_K, W)), whole((E, LANES))],
        out_shape=[jax.ShapeDtypeStruct((nt, TOP_K, W), jnp.int32),
                   jax.ShapeDtypeStruct((nt, TOP_K, W), F32),
                   jax.ShapeDtypeStruct((E, LANES), jnp.int32)],
        scratch_shapes=[pltpu.VMEM((nt, TOP_K, W), jnp.int32),
                        pltpu.VMEM((nt, TOP_K, W), F32)],
        compiler_params=pltpu.CompilerParams(vmem_limit_bytes=VMEM_LIMIT),
        name="route",
    )(logits_t)


def _work_items(counts):
    tm = MOE_TILE
    ends = jnp.cumsum(counts)
    starts = ends - counts
    first_blk = starts // tm
    last_blk = jnp.where(counts > 0, (ends - 1) // tm, first_blk - 1)
    n_items = last_blk - first_blk + 1
    item_end = jnp.cumsum(n_items)
    item_start = item_end - n_items
    return starts, ends, first_blk, item_start, item_end


def _item_table(counts, n_blocks):
    tm = MOE_TILE
    starts, ends, first_blk, item_start, item_end = _work_items(counts)
    n_slots = n_blocks + N_EXPERTS - 1
    j = jnp.arange(n_slots, dtype=jnp.int32)
    e = jnp.minimum(jnp.searchsorted(item_end, j, side='right'), N_EXPERTS - 1).astype(jnp.int32)
    real = j < item_end[-1]
    blk = jnp.where(real, first_blk[e] + j - item_start[e], n_blocks - 1)
    lo = jnp.where(real, jnp.maximum(starts[e], blk * tm) - blk * tm, 0)
    hi = jnp.where(real, jnp.minimum(ends[e], (blk + 1) * tm) - blk * tm, 0)
    last_e = e[jnp.maximum(item_end[-1] - 1, 0)]
    e = jnp.where(real, e, last_e)
    as_i32 = lambda a: a.astype(jnp.int32)
    return as_i32(blk), as_i32(e), as_i32(lo), as_i32(hi)


def _scatter_kernel(pos_ref, src_ref, dst_ref, sem):
    tg = src_ref.shape[0]
    for r in range(tg):
        for k in range(TOP_K):
            n = r * TOP_K + k
            pltpu.make_async_copy(src_ref.at[r], dst_ref.at[pos_ref[0, 0, n]], sem).start(priority=n % 2)
    for k in range(TOP_K):
        pltpu.make_async_copy(src_ref, dst_ref.at[pl.ds(0, tg)], sem).wait()


def _scatter_rows(src, pos):
    T = src.shape[0]
    tg = GATHER_TILE
    return pl.pallas_call(
        _scatter_kernel,
        grid=(T // tg,),
        in_specs=[pl.BlockSpec((1, 1, tg * TOP_K), lambda i: (i, 0, 0), memory_space=pltpu.SMEM),
                  pl.BlockSpec((tg, TILE_ROWS, LANES), lambda i: (i, 0, 0))],
        out_specs=pl.BlockSpec(memory_space=pl.ANY),
        out_shape=jax.ShapeDtypeStruct((T * TOP_K, TILE_ROWS, LANES), src.dtype),
        scratch_shapes=[pltpu.SemaphoreType.DMA(())],
        compiler_params=_params(("arbitrary",)),
        name="row_scatter",
    )(pos.reshape(T // tg, 1, tg * TOP_K), src)


def _moe_kernel(blk_ref, e_ref, lo_ref, hi_ref, xs_ref, wg_ref, bg_ref, wu_ref, bu_ref, wd_ref,
                bd_ref, y_ref, wg_s, wu_s, wd_s):
    j = pl.program_id(0)
    jp = jnp.maximum(j - 1, 0)
    tm = MOE_TILE

    @pl.when(jnp.logical_or(j == 0, e_ref[j] != e_ref[jp]))
    def _():
        rows = 128
        for c in range(D_MODEL // rows):
            sl = slice(c * rows, (c + 1) * rows)
            wg_s[sl, :] = wg_ref[sl, :].astype(BF16)
            wu_s[sl, :] = wu_ref[sl, :].astype(BF16)
            wd_s[sl, :] = wd_ref[sl, :].astype(BF16)

    lo = lo_ref[j]
    hi = hi_ref[j]

    first_visit = jnp.logical_or(j == 0, blk_ref[j] != blk_ref[jp])
    half = tm // 2

    def tile_rows(r0, n, c):
        return pl.ds(r0 * TILE_ROWS + c, n, stride=TILE_ROWS)

    def mlp(r0, n):
        x = jnp.concatenate([xs_ref[tile_rows(r0, n, c), :].astype(BF16) for c in range(TILE_ROWS)],
                            axis=1)
        y = None
        for f in range(D_FF // FF_CHUNK):
            cols = slice(f * FF_CHUNK, (f + 1) * FF_CHUNK)
            g = jnp.dot(x, wg_s[:, cols], preferred_element_type=F32) + bg_ref[:, cols]
            g = jnp.minimum(g, SWIGLU_LIMIT)
            u = jnp.dot(x, wu_s[:, cols], preferred_element_type=F32) + bu_ref[:, cols]
            u = jnp.clip(u, -SWIGLU_LIMIT, SWIGLU_LIMIT)
            act = (g * jax.nn.sigmoid(SWIGLU_ALPHA * g) * (u + 1.0)).astype(BF16)
            part = jnp.dot(act, wd_s[cols, :], preferred_element_type=F32)
            y = part if y is None else y + part
        return y + bd_ref[...]

    @pl.when(jnp.logical_and(lo == 0, hi == tm))
    def _():
        y = mlp(0, tm)
        for c in range(TILE_ROWS):
            y_ref[tile_rows(0, tm, c), :] = y[:, c * LANES:(c + 1) * LANES]

    def masked_rows(r0, n):
        y = mlp(r0, n)
        rows = r0 + lax.broadcasted_iota(jnp.int32, (n, LANES), 0)
        mine = jnp.logical_and(rows >= lo, rows < hi)

        @pl.when(first_visit)
        def _():
            for c in range(TILE_ROWS):
                y_ref[tile_rows(r0, n, c), :] = jnp.where(mine, y[:, c * LANES:(c + 1) * LANES], 0.0)

        @pl.when(jnp.logical_not(first_visit))
        def _():
            for c in range(TILE_ROWS):
                sl = tile_rows(r0, n, c)
                y_ref[sl, :] = jnp.where(mine, y[:, c * LANES:(c + 1) * LANES], y_ref[sl, :])

    partial = jnp.logical_and(lo < hi, jnp.logical_or(lo > 0, hi < tm))
    spans_both = jnp.logical_and(lo < half, hi > half)

    @pl.when(jnp.logical_and(partial, spans_both))
    def _():
        masked_rows(0, tm)

    for r0 in (0, half):
        inside = jnp.logical_and(lo >= r0, hi <= r0 + half)

        @pl.when(jnp.logical_and(partial, inside))
        def _():
            masked_rows(r0, half)

            @pl.when(first_visit)
            def _():
                other = half - r0
                y_ref[pl.ds(other * TILE_ROWS, half * TILE_ROWS), :] = jnp.zeros(
                    (half * TILE_ROWS, LANES), y_ref.dtype)


def _moe_experts(items, xs, wg, bg, wu, bu, wd, bd):
    P = xs.shape[0] // TILE_ROWS
    tm = MOE_TILE
    n_items = items[0].shape[0]
    tiles = pl.BlockSpec((tm * TILE_ROWS, LANES), lambda j, blk, e, lo, hi: (blk[j], 0))
    wmap = lambda j, blk, e, lo, hi: (e[j], 0, 0)
    wspec = pl.BlockSpec((None, D_MODEL, D_FF), wmap)
    bspec = pl.BlockSpec((None, 1, D_FF), wmap)
    grid_spec = pltpu.PrefetchScalarGridSpec(
        num_scalar_prefetch=4,
        grid=(n_items,),
        in_specs=[tiles, wspec, bspec, wspec, bspec, wspec, bspec],
        out_specs=tiles,
        scratch_shapes=[pltpu.VMEM((D_MODEL, D_FF), BF16),
                        pltpu.VMEM((D_MODEL, D_FF), BF16),
                        pltpu.VMEM((D_FF, D_MODEL), BF16)],
    )
    return pl.pallas_call(
        _moe_kernel,
        grid_spec=grid_spec,
        out_shape=jax.ShapeDtypeStruct((P * TILE_ROWS, LANES), F32),
        compiler_params=_params(("arbitrary",)),
        name="moe_experts",
    )(*items, xs, wg, bg, wu, bu, wd, bd)


def _combine_kernel(pos_ref, x1_ref, gt_ref, g_ref, ys_ref, o_ref, buf, sem):
    i = pl.program_id(0)
    n_tiles = pl.num_programs(0) - 1
    tm = x1_ref.shape[0]
    n_rows = TOP_K * tm

    @pl.when(i < n_tiles)
    def _():
        slot = i % 2
        for n in range(n_rows):
            src = ys_ref.at[pl.ds(pl.multiple_of(pos_ref[0, 0, n] * TILE_ROWS, TILE_ROWS), TILE_ROWS), :]
            pltpu.make_async_copy(src, buf.at[slot, pl.ds(n * TILE_ROWS, TILE_ROWS), :],
                                  sem.at[slot]).start(priority=n % 2)

    @pl.when(i > 0)
    def _():
        slot = (i - 1) % 2
        pltpu.make_async_copy(ys_ref.at[pl.ds(0, n_rows * TILE_ROWS), :], buf.at[slot], sem.at[slot]).wait()
        x = x1_ref[...]
        gates = gt_ref[...]
        for k in range(TOP_K):
            x = x + gates[:, k:k + 1] * _load_token_tiles(
                buf.at[slot, pl.ds(k * tm * TILE_ROWS, tm * TILE_ROWS), :], tm)
        ms = jnp.mean(x * x, axis=-1, keepdims=True)
        o_ref[...] = x * lax.rsqrt(ms + RMS_EPS) * g_ref[...]


def _combine(x1, ys, pos_tiles, gates, g):
    T = x1.shape[0]
    tm = COMBINE_TILE
    n_tiles = T // tm
    done = lambda i: (jnp.maximum(i - 1, 0), 0)
    return pl.pallas_call(
        _combine_kernel,
        grid=(n_tiles + 1,),
        in_specs=[pl.BlockSpec((1, 1, TOP_K * tm), lambda i: (jnp.minimum(i, n_tiles - 1), 0, 0),
                               memory_space=pltpu.SMEM),
                  pl.BlockSpec((tm, D_MODEL), done),
                  pl.BlockSpec((tm, TOP_K), done),
                  pl.BlockSpec((1, D_MODEL), lambda i: (0, 0)),
                  pl.BlockSpec(memory_space=pl.ANY)],
        out_specs=pl.BlockSpec((tm, D_MODEL), done),
        out_shape=jax.ShapeDtypeStruct((T, D_MODEL), F32),
        scratch_shapes=[pltpu.VMEM((2, TOP_K * tm * TILE_ROWS, LANES), F32),
                        pltpu.SemaphoreType.DMA((2,))],
        compiler_params=_params(("arbitrary",)),
        name="gather_combine_norm",
    )(pos_tiles, x1, gates, g, ys)


def kernel(x, norm1_g, w_in, lambda_q1, lambda_k1, lambda_q2, lambda_k2, subln_g, w_pool, b_pool,
           pool_scale, w_out, norm2_g, router_w, router_b, w_gate, b_gate, w_up, b_up, w_down,
           b_down, final_g):
    B, S, D = x.shape
    T = B * S
    l = 0
    lambda_init = 0.8 - 0.6 * math.exp(-0.3 * l)
    x2 = x.reshape(T, D)

    w = w_in[l]
    wqv_t = jnp.concatenate([w[:, :D_ATTN], w[:, 2 * D_ATTN:3 * D_ATTN]], axis=1).T.astype(BF16)
    qt, k, vt, u = _inproj(x2, norm1_g[l][None, :], wqv_t, w[:, D_ATTN:2 * D_ATTN].astype(BF16),
                           w[:, 3 * D_ATTN:].astype(BF16), B, S)
    attn = _attention(qt, k.reshape(B, S, D_ATTN), vt,
                      lambda_q1[l][None, :], lambda_k1[l][None, :],
                      lambda_q2[l][None, :], lambda_k2[l][None, :],
                      subln_g[l][None, :], lambda_init)
    pool = _pool(u.reshape(B, S, D_POOL), w_pool[l].astype(BF16),
                 b_pool[l].reshape(1, D_POOL), pool_scale[l][None, :])
    x1, h2, logits_t = _outproj(attn.reshape(T, D_ATTN), pool.reshape(T, D_POOL), x2,
                                w_out[l].astype(BF16), norm2_g[l][None, :],
                                router_w[l], router_b[l][None, :])

    pos_t, gates_t, counts = _route(logits_t)
    pos_tok = pos_t.transpose(0, 2, 1).reshape(T * TOP_K)
    nt, _, W = pos_t.shape
    pos_tiles = (pos_t.reshape(nt, TOP_K, W // COMBINE_TILE, COMBINE_TILE).transpose(0, 2, 1, 3)
                 .reshape(T // COMBINE_TILE, 1, TOP_K * COMBINE_TILE))
    gates = gates_t.transpose(0, 2, 1).reshape(T, TOP_K)
    assert (T * TOP_K) % MOE_TILE == 0
    items = _item_table(counts[:, 0], T * TOP_K // MOE_TILE)

    xs = _scatter_rows(h2.reshape(T, TILE_ROWS, LANES), pos_tok)
    ys = _moe_experts(items, xs.reshape(-1, LANES),
                      w_gate[l], b_gate[l][:, None, :], w_up[l], b_up[l][:, None, :],
                      w_down[l], b_down[l][:, None, :])
    out = _combine(x1, ys, pos_tiles, gates, final_g[None, :])
    return out.reshape(B, S, D)
```

```python
import functools
import math

import jax
import jax.numpy as jnp
from jax import lax
from jax.experimental import pallas as pl
from jax.experimental.pallas import tpu as pltpu

D_MODEL = 1024
D_ATTN = 512
D_POOL = 512
N_DIFF_HEADS = 4
DIFF_QK_DIM = 64
DIFF_V_DIM = 128
POOL_WINDOWS = (2, 4, 8, 16)
POOL_GROUP_DIM = 128
N_EXPERTS = 32
TOP_K = 4
D_FF = 1024
SWIGLU_LIMIT = 7.0
SWIGLU_ALPHA = 1.702
RMS_EPS = 1e-5

F32 = jnp.float32
BF16 = jnp.bfloat16

TOKEN_TILE = 512
ATTN_TILE = 256
MOE_TILE = 512
GATHER_TILE = 256
COMBINE_TILE = 256
FF_CHUNK = 512
VMEM_LIMIT = 56 * 1024 * 1024


def _params(sem, vmem=VMEM_LIMIT):
    return pltpu.CompilerParams(dimension_semantics=sem, vmem_limit_bytes=vmem)


LANES = 128
TILE_ROWS = D_MODEL // LANES
assert TILE_ROWS == 8


def _store_token_tiles(ref, x):
    n = x.shape[0]
    for c in range(TILE_ROWS):
        ref[pl.ds(c, n, stride=TILE_ROWS), :] = x[:, c * LANES:(c + 1) * LANES]


def _load_token_tiles(ref, n, dtype=None):
    cols = [ref[pl.ds(c, n, stride=TILE_ROWS), :] for c in range(TILE_ROWS)]
    if dtype is not None:
        cols = [c.astype(dtype) for c in cols]
    return jnp.concatenate(cols, axis=1)


def _inproj_kernel(x_ref, g_ref, wqv_ref, wk_ref, wu_ref, qt_ref, k_ref, vt_ref, u_ref):
    x = x_ref[...]
    ms = jnp.mean(x * x, axis=-1, keepdims=True)
    h = (x * lax.rsqrt(ms + RMS_EPS) * g_ref[...]).astype(BF16)
    qvt = lax.dot_general(wqv_ref[...], h, (((1,), (1,)), ((), ())), preferred_element_type=F32)
    qt_ref[...] = (qvt[:D_ATTN] * (DIFF_QK_DIM ** -0.5 * math.log2(math.e))).astype(BF16)
    vt_ref[...] = qvt[D_ATTN:].astype(BF16)
    k_ref[...] = jnp.dot(h, wk_ref[...], preferred_element_type=F32).astype(BF16)
    u_ref[...] = jnp.dot(h, wu_ref[...], preferred_element_type=F32)


def _inproj(x2, g, wqv_t, wk, wu, B, S):
    T = x2.shape[0]
    tm = TOKEN_TILE
    per_seq = S // tm
    row = lambda i: (i, 0)
    fixed = lambda i: (0, 0)
    tmap = lambda i: (i // per_seq, 0, i % per_seq)
    return pl.pallas_call(
        _inproj_kernel,
        grid=(T // tm,),
        in_specs=[pl.BlockSpec((tm, D_MODEL), row),
                  pl.BlockSpec((1, D_MODEL), fixed),
                  pl.BlockSpec((2 * D_ATTN, D_MODEL), fixed),
                  pl.BlockSpec((D_MODEL, D_ATTN), fixed),
                  pl.BlockSpec((D_MODEL, D_POOL), fixed)],
        out_specs=[pl.BlockSpec((None, D_ATTN, tm), tmap),
                   pl.BlockSpec((tm, D_ATTN), row),
                   pl.BlockSpec((None, D_ATTN, tm), tmap),
                   pl.BlockSpec((tm, D_POOL), row)],
        out_shape=[jax.ShapeDtypeStruct((B, D_ATTN, S), BF16),
                   jax.ShapeDtypeStruct((T, D_ATTN), BF16),
                   jax.ShapeDtypeStruct((B, D_ATTN, S), BF16),
                   jax.ShapeDtypeStruct((T, D_POOL), F32)],
        compiler_params=_params(("arbitrary",)),
        name="inproj",
    )(x2, g, wqv_t, wk, wu)


def _attn_kernel(qa_ref, qb_ref, k_ref, vt_ref, lq1_ref, lk1_ref, lq2_ref, lk2_ref, sg_ref,
                 o_ref, qs_s, vt_s, m_s, acc_s, s_s, *, lambda_init, n_q):
    tq = qa_ref.shape[1]
    tk = tq
    dv = DIFF_V_DIM
    p = pl.program_id(2)

    @pl.when(p == 0)
    def _():
        for c in range(n_q):
            vt_s[c, 0:dv, :] = vt_ref[:, c * tk:(c + 1) * tk]
            vt_s[c, dv:, :] = jnp.ones((vt_s.shape[1] - dv, tk), BF16)

    feat = lax.broadcasted_iota(jnp.int32, (dv, tq), 0)
    for blk, q_ref in enumerate((qa_ref, qb_ref)):
        qt = q_ref[...]
        zero = jnp.zeros_like(qt)
        qs_s[blk] = jnp.concatenate([jnp.where(feat < DIFF_QK_DIM, qt, zero),
                                     jnp.where(feat >= DIFF_QK_DIM, qt, zero)], axis=1)
    m_s[...] = jnp.full(m_s.shape, -jnp.inf, F32)
    acc_s[...] = jnp.zeros(acc_s.shape, F32)

    key = lax.broadcasted_iota(jnp.int32, (tk, tq), 0)
    qry = lax.broadcasted_iota(jnp.int32, (tk, tq), 1)
    bias = jnp.where(key <= qry, 0.0, -jnp.inf).astype(F32)
    bias = jnp.concatenate([bias, bias], axis=1)

    items = [(0, p, True), (1, n_q - 1 - p, True)]
    for n in range(n_q - 1):
        in_a = n < p
        items.append((jnp.where(in_a, 0, 1), jnp.where(in_a, n, n - p), False))

    def scores(blk, chunk, diag):
        kc = k_ref[pl.ds(pl.multiple_of(chunk * tk, tk), tk), :]
        s = jnp.dot(kc, qs_s[blk], preferred_element_type=F32)
        return s + bias if diag else s

    for t, (blk, chunk, diag) in enumerate(items):
        s = scores(blk, chunk, diag)
        s_s[t] = s
        m_s[blk] = jnp.maximum(m_s[blk], jnp.max(s, axis=0, keepdims=True))

    for t, (blk, chunk, diag) in enumerate(items):
        pt = jnp.exp2(s_s[t] - m_s[blk]).astype(BF16)
        acc_s[blk] += jnp.dot(vt_s[chunk], pt, preferred_element_type=F32)

    lam = (jnp.exp(jnp.sum(lq1_ref[...] * lk1_ref[...]))
           - jnp.exp(jnp.sum(lq2_ref[...] * lk2_ref[...])) + lambda_init)
    for blk, qblock in enumerate((p, n_q - 1 - p)):
        acc = acc_s[blk]
        o0 = acc[0:dv, 0:tq] / acc[dv:dv + 1, 0:tq]
        o1 = acc[0:dv, tq:] / acc[dv:dv + 1, tq:]
        a = o0 - lam * o1
        ms = jnp.mean(a * a, axis=0, keepdims=True)
        y = (a * lax.rsqrt(ms + RMS_EPS)).T * sg_ref[...]
        o_ref[pl.ds(pl.multiple_of(qblock * tq, tq), tq), :] = (y * (1.0 - lambda_init)).astype(o_ref.dtype)


def _attention(qt, k, vt, lq1, lk1, lq2, lk2, subln_g, lambda_init):
    B, _, S = qt.shape
    tq = ATTN_TILE
    n_q = S // tq
    assert S % tq == 0 and n_q % 2 == 0
    ones_rows = 16
    vec = lambda n: pl.BlockSpec((1, n), lambda b, h, p: (0, 0))
    return pl.pallas_call(
        functools.partial(_attn_kernel, lambda_init=lambda_init, n_q=n_q),
        grid=(B, N_DIFF_HEADS, n_q // 2),
        in_specs=[pl.BlockSpec((None, DIFF_V_DIM, tq), lambda b, h, p: (b, h, p)),
                  pl.BlockSpec((None, DIFF_V_DIM, tq), lambda b, h, p: (b, h, n_q - 1 - p)),
                  pl.BlockSpec((None, S, DIFF_V_DIM), lambda b, h, p: (b, 0, h)),
                  pl.BlockSpec((None, DIFF_V_DIM, S), lambda b, h, p: (b, h, 0)),
                  vec(DIFF_QK_DIM), vec(DIFF_QK_DIM), vec(DIFF_QK_DIM), vec(DIFF_QK_DIM),
                  vec(DIFF_V_DIM)],
        out_specs=pl.BlockSpec((None, S, DIFF_V_DIM), lambda b, h, p: (b, 0, h)),
        out_shape=jax.ShapeDtypeStruct((B, S, D_ATTN), BF16),
        scratch_shapes=[pltpu.VMEM((2, DIFF_V_DIM, 2 * tq), BF16),
                        pltpu.VMEM((n_q, DIFF_V_DIM + ones_rows, tq), BF16),
                        pltpu.VMEM((2, 1, 2 * tq), F32),
                        pltpu.VMEM((2, DIFF_V_DIM + ones_rows, 2 * tq), F32),
                        pltpu.VMEM((n_q + 1, tq, 2 * tq), F32)],
        compiler_params=_params(("arbitrary", "arbitrary", "arbitrary")),
        name="diff_attn",
    )(qt, qt, k, vt, lq1, lk1, lq2, lk2, subln_g)


def _pool_kernel(u_ref, w_ref, b_ref, sc_ref, o_ref):
    S = u_ref.shape[0]
    row = lax.broadcasted_iota(jnp.int32, (S, POOL_GROUP_DIM), 0)
    for g, w in enumerate(POOL_WINDOWS):
        cols = slice(g * POOL_GROUP_DIM, (g + 1) * POOL_GROUP_DIM)
        ug = u_ref[:, cols]
        s = ug
        span = 1
        while span < w:
            shifted = pltpu.roll(s, shift=span, axis=0)
            s = s + jnp.where(row >= span, shifted, 0.0)
            span *= 2
        cnt = jnp.minimum(row + 1, w).astype(F32)
        z = (s / cnt - ug).astype(BF16)
        y = jnp.dot(z, w_ref[g], preferred_element_type=F32) + b_ref[:, cols]
        o_ref[:, cols] = (y * sc_ref[:, cols]).astype(o_ref.dtype)


def _pool(u, w_pool, b_pool, pool_scale):
    B, S, _ = u.shape
    blk = pl.BlockSpec((None, S, D_POOL), lambda b: (b, 0, 0))
    return pl.pallas_call(
        _pool_kernel,
        grid=(B,),
        in_specs=[blk,
                  pl.BlockSpec((len(POOL_WINDOWS), POOL_GROUP_DIM, POOL_GROUP_DIM), lambda b: (0, 0, 0)),
                  pl.BlockSpec((1, D_POOL), lambda b: (0, 0)),
                  pl.BlockSpec((1, D_POOL), lambda b: (0, 0))],
        out_specs=blk,
        out_shape=jax.ShapeDtypeStruct((B, S, D_POOL), BF16),
        compiler_params=_params(("arbitrary",)),
        name="pool_mixer",
    )(u, w_pool, b_pool, pool_scale)


def _outproj_kernel(a_ref, p_ref, x_ref, w_ref, g_ref, rw_ref, rb_ref, x1_ref, h_ref, lg_ref):
    x1 = (x_ref[...]
          + jnp.dot(a_ref[...], w_ref[0:D_ATTN, :], preferred_element_type=F32)
          + jnp.dot(p_ref[...], w_ref[D_ATTN:, :], preferred_element_type=F32))
    x1_ref[...] = x1
    ms = jnp.mean(x1 * x1, axis=-1, keepdims=True)
    h = x1 * lax.rsqrt(ms + RMS_EPS) * g_ref[...]
    _store_token_tiles(h_ref, h)
    h_hi = h.astype(BF16)
    h_lo = (h - h_hi.astype(F32)).astype(BF16)
    both = jnp.dot(h_hi, rw_ref[...], preferred_element_type=F32)
    lg = (both[:, :LANES] + both[:, LANES:]
          + jnp.dot(h_lo, rw_ref[:, :LANES], preferred_element_type=F32) + rb_ref[...])
    lg_ref[...] = lg.T[:N_EXPERTS]


def _outproj(attn, pool, x2, w_out, g2, rw, rb):
    T = x2.shape[0]
    tm = TOKEN_TILE
    row = lambda i: (i, 0)
    fixed = lambda i: (0, 0)
    pad = ((0, 0), (0, LANES - N_EXPERTS))
    rw_hi = rw.astype(BF16)
    rw_lo = (rw - rw_hi.astype(F32)).astype(BF16)
    rw = jnp.concatenate([jnp.pad(rw_hi, pad), jnp.pad(rw_lo, pad)], axis=1)
    rb = jnp.pad(rb, pad)
    return pl.pallas_call(
        _outproj_kernel,
        grid=(T // tm,),
        in_specs=[pl.BlockSpec((tm, D_ATTN), row),
                  pl.BlockSpec((tm, D_POOL), row),
                  pl.BlockSpec((tm, D_MODEL), row),
                  pl.BlockSpec((D_MODEL, D_MODEL), fixed),
                  pl.BlockSpec((1, D_MODEL), fixed),
                  pl.BlockSpec((D_MODEL, 2 * LANES), fixed),
                  pl.BlockSpec((1, LANES), fixed)],
        out_specs=[pl.BlockSpec((tm, D_MODEL), row),
                   pl.BlockSpec((tm * TILE_ROWS, LANES), row),
                   pl.BlockSpec((None, N_EXPERTS, tm), lambda i: (i, 0, 0))],
        out_shape=[jax.ShapeDtypeStruct((T, D_MODEL), F32),
                   jax.ShapeDtypeStruct((T * TILE_ROWS, LANES), F32),
                   jax.ShapeDtypeStruct((T // tm, N_EXPERTS, tm), F32)],
        compiler_params=_params(("arbitrary",)),
        name="outproj_router",
    )(attn, pool, x2, w_out, g2, rw, rb)


def _route_kernel(lt_ref, pos_ref, gate_ref, cnt_ref, idx_s, rank_s):
    nt, E, W = lt_ref.shape
    e_iota = lax.broadcasted_iota(jnp.int32, (E, W), 0)
    before = (lax.broadcasted_iota(jnp.int32, (W, W), 0)
              < lax.broadcasted_iota(jnp.int32, (W, W), 1)).astype(BF16)
    ones = jnp.ones((W, LANES), BF16)
    widen = lambda a: jnp.concatenate([a] * (W // LANES), axis=1)

    def phase1(i, running):
        v = lt_ref[i]
        sel = jnp.zeros((E, W), F32)
        tops, hots = [], []
        for k in range(TOP_K):
            m = jnp.max(v, axis=0, keepdims=True)
            idx = jnp.min(jnp.where(v == m, e_iota, E), axis=0, keepdims=True)
            hot = e_iota == idx
            v = jnp.where(hot, -jnp.inf, v)
            sel = sel + hot.astype(F32)
            idx_s[i, k:k + 1, :] = idx
            tops.append(m)
            hots.append(hot)
        selb = sel.astype(BF16)
        rank = jnp.dot(selb, before, preferred_element_type=F32) + widen(running)
        for k in range(TOP_K):
            rank_s[i, k:k + 1, :] = jnp.sum(jnp.where(hots[k], rank, 0.0), axis=0, keepdims=True)
        ex = [jnp.exp(t - tops[0]) for t in tops]
        den = ex[0] + ex[1] + ex[2] + ex[3]
        for k in range(TOP_K):
            gate_ref[i, k:k + 1, :] = ex[k] / den
        return running + jnp.dot(selb, ones, preferred_element_type=F32)

    counts = lax.fori_loop(0, nt, phase1, jnp.zeros((E, LANES), F32))
    cnt_ref[...] = counts.astype(jnp.int32)

    row = lax.broadcasted_iota(jnp.int32, (E, LANES), 0)
    incl = counts
    span = 1
    while span < E:
        incl = incl + jnp.where(row >= span, pltpu.roll(incl, shift=span, axis=0), 0.0)
        span *= 2
    start = widen(incl - counts)

    def phase2(i, c):
        for k in range(TOP_K):
            hot = e_iota == idx_s[i, k:k + 1, :]
            base = jnp.sum(jnp.where(hot, start, 0.0), axis=0, keepdims=True)
            pos_ref[i, k:k + 1, :] = (base + rank_s[i, k:k + 1, :]).astype(jnp.int32)
        return c

    lax.fori_loop(0, nt, phase2, 0)


def _route(logits_t):
    nt, E, W = logits_t.shape
    whole = lambda shape: pl.BlockSpec(shape, lambda: (0,) * len(shape))
    return pl.pallas_call(
        _route_kernel,
        in_specs=[whole((nt, E, W))],
        out_specs=[whole((nt, TOP_K, W)), whole((nt, TOP_K, W)), whole((E, LANES))],
        out_shape=[jax.ShapeDtypeStruct((nt, TOP_K, W), jnp.int32),
                   jax.ShapeDtypeStruct((nt, TOP_K, W), F32),
                   jax.ShapeDtypeStruct((E, LANES), jnp.int32)],
        scratch_shapes=[pltpu.VMEM((nt, TOP_K, W), jnp.int32),
                        pltpu.VMEM((nt, TOP_K, W), F32)],
        compiler_params=pltpu.CompilerParams(vmem_limit_bytes=VMEM_LIMIT),
        name="route",
    )(logits_t)


def _work_items(counts):
    tm = MOE_TILE
    ends = jnp.cumsum(counts)
    starts = ends - counts
    first_blk = starts // tm
    last_blk = jnp.where(counts > 0, (ends - 1) // tm, first_blk - 1)
    n_items = last_blk - first_blk + 1
    item_end = jnp.cumsum(n_items)
    item_start = item_end - n_items
    return starts, ends, first_blk, item_start, item_end


def _item_table(counts, n_blocks):
    tm = MOE_TILE
    starts, ends, first_blk, item_start, item_end = _work_items(counts)
    n_slots = n_blocks + N_EXPERTS - 1
    j = jnp.arange(n_slots, dtype=jnp.int32)
    e = jnp.minimum(jnp.sum(item_end[None, :] <= j[:, None], axis=1), N_EXPERTS - 1).astype(jnp.int32)
    real = j < item_end[-1]
    blk = jnp.where(real, first_blk[e] + j - item_start[e], n_blocks - 1)
    lo = jnp.where(real, jnp.maximum(starts[e], blk * tm) - blk * tm, 0)
    hi = jnp.where(real, jnp.minimum(ends[e], (blk + 1) * tm) - blk * tm, 0)
    last_e = e[jnp.maximum(item_end[-1] - 1, 0)]
    e = jnp.where(real, e, last_e)
    ids = jnp.arange(N_EXPERTS, dtype=jnp.int32)
    used = counts > 0
    order = jnp.cumsum(used.astype(jnp.int32)) - 1
    later = jnp.where(jnp.logical_and(used[None, :], ids[None, :] > ids[:, None]), ids[None, :], N_EXPERTS)
    nxt_of = jnp.min(later, axis=1)
    nxt_of = jnp.where(nxt_of == N_EXPERTS, -1, nxt_of)
    as_i32 = lambda a: a.astype(jnp.int32)
    return as_i32(blk), as_i32(e), as_i32(lo), as_i32(hi), as_i32(order[e] % 2), as_i32(nxt_of[e])


def _scatter_kernel(pos_ref, src_ref, dst_ref, sem):
    tg = src_ref.shape[0]
    for r in range(tg):
        for k in range(TOP_K):
            pltpu.make_async_copy(src_ref.at[r], dst_ref.at[pos_ref[0, 0, k * tg + r]],
                                  sem).start(priority=(r * TOP_K + k) % 2)
    for k in range(TOP_K):
        pltpu.make_async_copy(src_ref, dst_ref.at[pl.ds(0, tg)], sem).wait()


def _scatter_rows(src, pos_tiles):
    T = src.shape[0]
    tg = GATHER_TILE
    return pl.pallas_call(
        _scatter_kernel,
        grid=(T // tg,),
        in_specs=[pl.BlockSpec((1, 1, tg * TOP_K), lambda i: (i, 0, 0), memory_space=pltpu.SMEM),
                  pl.BlockSpec((tg, TILE_ROWS, LANES), lambda i: (i, 0, 0))],
        out_specs=pl.BlockSpec(memory_space=pl.ANY),
        out_shape=jax.ShapeDtypeStruct((T * TOP_K, TILE_ROWS, LANES), src.dtype),
        scratch_shapes=[pltpu.SemaphoreType.DMA(())],
        compiler_params=_params(("arbitrary",)),
        name="row_scatter",
    )(pos_tiles, src)


def _moe_kernel(blk_ref, e_ref, lo_ref, hi_ref, par_ref, nxt_ref, xs_ref, wg_hbm, bg_ref, wu_hbm,
                bu_ref, wd_hbm, bd_ref, y_ref, wg_s, wu_s, wd_s, wbuf, wsem):
    j = pl.program_id(0)
    jp = jnp.maximum(j - 1, 0)
    tm = MOE_TILE
    srcs = (wg_hbm, wu_hbm, wd_hbm)

    def weight_copies(expert, buffer):
        return [pltpu.make_async_copy(src.at[expert], wbuf.at[buffer, m], wsem.at[buffer])
                for m, src in enumerate(srcs)]

    @pl.when(j == 0)
    def _():
        for cp in weight_copies(e_ref[0], par_ref[0]):
            cp.start()

    @pl.when(jnp.logical_or(j == 0, e_ref[j] != e_ref[jp]))
    def _():
        buffer = par_ref[j]
        for cp in weight_copies(e_ref[j], buffer):
            cp.wait()
        rows = 128
        for c in range(D_MODEL // rows):
            sl = slice(c * rows, (c + 1) * rows)
            wg_s[sl, :] = wbuf[buffer, 0, sl, :].astype(BF16)
            wu_s[sl, :] = wbuf[buffer, 1, sl, :].astype(BF16)
            wd_s[sl, :] = wbuf[buffer, 2, sl, :].astype(BF16)

        @pl.when(nxt_ref[j] >= 0)
        def _():
            for cp in weight_copies(nxt_ref[j], 1 - buffer):
                cp.start()

    lo = lo_ref[j]
    hi = hi_ref[j]

    first_visit = jnp.logical_or(j == 0, blk_ref[j] != blk_ref[jp])
    half = tm // 2

    def tile_rows(r0, n, c):
        return pl.ds(r0 * TILE_ROWS + c, n, stride=TILE_ROWS)

    def mlp(r0, n):
        x = jnp.concatenate([xs_ref[tile_rows(r0, n, c), :].astype(BF16) for c in range(TILE_ROWS)],
                            axis=1)
        y = None
        for f in range(D_FF // FF_CHUNK):
            cols = slice(f * FF_CHUNK, (f + 1) * FF_CHUNK)
            g = jnp.dot(x, wg_s[:, cols], preferred_element_type=F32) + bg_ref[:, cols]
            g = jnp.minimum(g, SWIGLU_LIMIT)
            u = jnp.dot(x, wu_s[:, cols], preferred_element_type=F32) + bu_ref[:, cols]
            u = jnp.clip(u, -SWIGLU_LIMIT, SWIGLU_LIMIT)
            act = (g * jax.nn.sigmoid(SWIGLU_ALPHA * g) * (u + 1.0)).astype(BF16)
            part = jnp.dot(act, wd_s[cols, :], preferred_element_type=F32)
            y = part if y is None else y + part
        return y + bd_ref[...]

    @pl.when(jnp.logical_and(lo == 0, hi == tm))
    def _():
        y = mlp(0, tm)
        for c in range(TILE_ROWS):
            y_ref[tile_rows(0, tm, c), :] = y[:, c * LANES:(c + 1) * LANES]

    def masked_rows(r0, n):
        y = mlp(r0, n)
        rows = r0 + lax.broadcasted_iota(jnp.int32, (n, LANES), 0)
        mine = jnp.logical_and(rows >= lo, rows < hi)

        @pl.when(first_visit)
        def _():
            for c in range(TILE_ROWS):
                y_ref[tile_rows(r0, n, c), :] = jnp.where(mine, y[:, c * LANES:(c + 1) * LANES], 0.0)

        @pl.when(jnp.logical_not(first_visit))
        def _():
            for c in range(TILE_ROWS):
                sl = tile_rows(r0, n, c)
                y_ref[sl, :] = jnp.where(mine, y[:, c * LANES:(c + 1) * LANES], y_ref[sl, :])

    partial = jnp.logical_and(lo < hi, jnp.logical_or(lo > 0, hi < tm))
    spans_both = jnp.logical_and(lo < half, hi > half)

    @pl.when(jnp.logical_and(partial, spans_both))
    def _():
        masked_rows(0, tm)

    for r0 in (0, half):
        inside = jnp.logical_and(lo >= r0, hi <= r0 + half)

        @pl.when(jnp.logical_and(partial, inside))
        def _():
            masked_rows(r0, half)

            @pl.when(first_visit)
            def _():
                other = half - r0
                y_ref[pl.ds(other * TILE_ROWS, half * TILE_ROWS), :] = jnp.zeros(
                    (half * TILE_ROWS, LANES), y_ref.dtype)


def _moe_experts(items, xs, wg, bg, wu, bu, wd, bd):
    P = xs.shape[0] // TILE_ROWS
    tm = MOE_TILE
    n_items = items[0].shape[0]
    assert D_FF == D_MODEL
    tiles = pl.BlockSpec((tm * TILE_ROWS, LANES), lambda j, blk, e, lo, hi, par, nxt: (blk[j], 0))
    wspec = pl.BlockSpec(memory_space=pl.ANY)
    bspec = pl.BlockSpec((None, 1, D_FF), lambda j, blk, e, lo, hi, par, nxt: (e[j], 0, 0))
    grid_spec = pltpu.PrefetchScalarGridSpec(
        num_scalar_prefetch=6,
        grid=(n_items,),
        in_specs=[tiles, wspec, bspec, wspec, bspec, wspec, bspec],
        out_specs=tiles,
        scratch_shapes=[pltpu.VMEM((D_MODEL, D_FF), BF16),
                        pltpu.VMEM((D_MODEL, D_FF), BF16),
                        pltpu.VMEM((D_FF, D_MODEL), BF16),
                        pltpu.VMEM((2, 3, D_MODEL, D_FF), F32),
                        pltpu.SemaphoreType.DMA((2,))],
    )
    return pl.pallas_call(
        _moe_kernel,
        grid_spec=grid_spec,
        out_shape=jax.ShapeDtypeStruct((P * TILE_ROWS, LANES), F32),
        compiler_params=_params(("arbitrary",)),
        name="moe_experts",
    )(*items, xs, wg, bg, wu, bu, wd, bd)


def _combine_kernel(pos_ref, x1_ref, gt_ref, g_ref, ys_ref, o_ref, buf, sem):
    i = pl.program_id(0)
    n_tiles = pl.num_programs(0) - 1
    tm = x1_ref.shape[0]
    n_rows = TOP_K * tm

    @pl.when(i < n_tiles)
    def _():
        slot = i % 2
        for n in range(n_rows):
            src = ys_ref.at[pl.ds(pl.multiple_of(pos_ref[0, 0, n] * TILE_ROWS, TILE_ROWS), TILE_ROWS), :]
            pltpu.make_async_copy(src, buf.at[slot, pl.ds(n * TILE_ROWS, TILE_ROWS), :],
                                  sem.at[slot]).start(priority=n % 2)

    @pl.when(i > 0)
    def _():
        slot = (i - 1) % 2
        pltpu.make_async_copy(ys_ref.at[pl.ds(0, n_rows * TILE_ROWS), :], buf.at[slot], sem.at[slot]).wait()
        x = x1_ref[...]
        gates = gt_ref[...]
        for k in range(TOP_K):
            x = x + gates[:, k:k + 1] * _load_token_tiles(
                buf.at[slot, pl.ds(k * tm * TILE_ROWS, tm * TILE_ROWS), :], tm)
        ms = jnp.mean(x * x, axis=-1, keepdims=True)
        o_ref[...] = x * lax.rsqrt(ms + RMS_EPS) * g_ref[...]


def _combine(x1, ys, pos_tiles, gates, g):
    T = x1.shape[0]
    tm = COMBINE_TILE
    n_tiles = T // tm
    done = lambda i: (jnp.maximum(i - 1, 0), 0)
    return pl.pallas_call(
        _combine_kernel,
        grid=(n_tiles + 1,),
        in_specs=[pl.BlockSpec((1, 1, TOP_K * tm), lambda i: (jnp.minimum(i, n_tiles - 1), 0, 0),
                               memory_space=pltpu.SMEM),
                  pl.BlockSpec((tm, D_MODEL), done),
                  pl.BlockSpec((tm, TOP_K), done),
                  pl.BlockSpec((1, D_MODEL), lambda i: (0, 0)),
                  pl.BlockSpec(memory_space=pl.ANY)],
        out_specs=pl.BlockSpec((tm, D_MODEL), done),
        out_shape=jax.ShapeDtypeStruct((T, D_MODEL), F32),
        scratch_shapes=[pltpu.VMEM((2, TOP_K * tm * TILE_ROWS, LANES), F32),
                        pltpu.SemaphoreType.DMA((2,))],
        compiler_params=_params(("arbitrary",)),
        name="gather_combine_norm",
    )(pos_tiles, x1, gates, g, ys)


def kernel(x, norm1_g, w_in, lambda_q1, lambda_k1, lambda_q2, lambda_k2, subln_g, w_pool, b_pool,
           pool_scale, w_out, norm2_g, router_w, router_b, w_gate, b_gate, w_up, b_up, w_down,
           b_down, final_g):
    B, S, D = x.shape
    T = B * S
    l = 0
    lambda_init = 0.8 - 0.6 * math.exp(-0.3 * l)
    x2 = x.reshape(T, D)

    w = w_in[l]
    wqv_t = jnp.concatenate([w[:, :D_ATTN], w[:, 2 * D_ATTN:3 * D_ATTN]], axis=1).T.astype(BF16)
    qt, k, vt, u = _inproj(x2, norm1_g[l][None, :], wqv_t, w[:, D_ATTN:2 * D_ATTN].astype(BF16),
                           w[:, 3 * D_ATTN:].astype(BF16), B, S)
    attn = _attention(qt, k.reshape(B, S, D_ATTN), vt,
                      lambda_q1[l][None, :], lambda_k1[l][None, :],
                      lambda_q2[l][None, :], lambda_k2[l][None, :],
                      subln_g[l][None, :], lambda_init)
    pool = _pool(u.reshape(B, S, D_POOL), w_pool[l].astype(BF16),
                 b_pool[l].reshape(1, D_POOL), pool_scale[l][None, :])
    x1, h2, logits_t = _outproj(attn.reshape(T, D_ATTN), pool.reshape(T, D_POOL), x2,
                                w_out[l].astype(BF16), norm2_g[l][None, :],
                                router_w[l], router_b[l][None, :])

    pos_t, gates_t, counts = _route(logits_t)
    nt, _, W = pos_t.shape
    assert GATHER_TILE == COMBINE_TILE
    pos_tiles = (pos_t.reshape(nt, TOP_K, W // COMBINE_TILE, COMBINE_TILE).transpose(0, 2, 1, 3)
                 .reshape(T // COMBINE_TILE, 1, TOP_K * COMBINE_TILE))
    gates = gates_t.transpose(0, 2, 1).reshape(T, TOP_K)
    assert (T * TOP_K) % MOE_TILE == 0
    items = _item_table(counts[:, 0], T * TOP_K // MOE_TILE)

    xs = _scatter_rows(h2.reshape(T, TILE_ROWS, LANES), pos_tiles)
    ys = _moe_experts(items, xs.reshape(-1, LANES),
                      w_gate[l], b_gate[l][:, None, :], w_up[l], b_up[l][:, None, :],
                      w_down[l], b_down[l][:, None, :])
    out = _combine(x1, ys, pos_tiles, gates, final_g[None, :])
    return out.reshape(B, S, D)
```

```python
import functools
import math

import jax
import jax.numpy as jnp
from jax import lax
from jax.experimental import pallas as pl
from jax.experimental.pallas import tpu as pltpu

D_MODEL = 1024
D_ATTN = 512
D_POOL = 512
N_DIFF_HEADS = 4
DIFF_QK_DIM = 64
DIFF_V_DIM = 128
POOL_WINDOWS = (2, 4, 8, 16)
POOL_GROUP_DIM = 128
N_EXPERTS = 32
TOP_K = 4
D_FF = 1024
SWIGLU_LIMIT = 7.0
SWIGLU_ALPHA = 1.702
RMS_EPS = 1e-5

F32 = jnp.float32
BF16 = jnp.bfloat16

TOKEN_TILE = 512
ATTN_TILE = 256
MOE_TILE = 512
GATHER_TILE = 256
COMBINE_TILE = 256
FF_CHUNK = 512
VMEM_LIMIT = 56 * 1024 * 1024


def _params(sem, vmem=VMEM_LIMIT):
    return pltpu.CompilerParams(dimension_semantics=sem, vmem_limit_bytes=vmem)


LANES = 128
TILE_ROWS = D_MODEL // LANES
assert TILE_ROWS == 8


def _store_token_tiles(ref, x):
    n = x.shape[0]
    for c in range(TILE_ROWS):
        ref[pl.ds(c, n, stride=TILE_ROWS), :] = x[:, c * LANES:(c + 1) * LANES]


def _load_token_tiles(ref, n, dtype=None):
    cols = [ref[pl.ds(c, n, stride=TILE_ROWS), :] for c in range(TILE_ROWS)]
    if dtype is not None:
        cols = [c.astype(dtype) for c in cols]
    return jnp.concatenate(cols, axis=1)


def _inproj_kernel(x_ref, g_ref, wqv_ref, wk_ref, wu_ref, qt_ref, k_ref, vt_ref, u_ref):
    x = x_ref[...]
    ms = jnp.mean(x * x, axis=-1, keepdims=True)
    h = (x * lax.rsqrt(ms + RMS_EPS) * g_ref[...]).astype(BF16)
    qvt = lax.dot_general(wqv_ref[...], h, (((1,), (1,)), ((), ())), preferred_element_type=F32)
    qt_ref[...] = (qvt[:D_ATTN] * (DIFF_QK_DIM ** -0.5 * math.log2(math.e))).astype(BF16)
    vt_ref[...] = qvt[D_ATTN:].astype(BF16)
    k_ref[...] = jnp.dot(h, wk_ref[...], preferred_element_type=F32).astype(BF16)
    u_ref[...] = jnp.dot(h, wu_ref[...], preferred_element_type=F32)


def _inproj(x2, g, wqv_t, wk, wu, B, S):
    T = x2.shape[0]
    tm = TOKEN_TILE
    per_seq = S // tm
    row = lambda i: (i, 0)
    fixed = lambda i: (0, 0)
    tmap = lambda i: (i // per_seq, 0, i % per_seq)
    return pl.pallas_call(
        _inproj_kernel,
        grid=(T // tm,),
        in_specs=[pl.BlockSpec((tm, D_MODEL), row),
                  pl.BlockSpec((1, D_MODEL), fixed),
                  pl.BlockSpec((2 * D_ATTN, D_MODEL), fixed),
                  pl.BlockSpec((D_MODEL, D_ATTN), fixed),
                  pl.BlockSpec((D_MODEL, D_POOL), fixed)],
        out_specs=[pl.BlockSpec((None, D_ATTN, tm), tmap),
                   pl.BlockSpec((tm, D_ATTN), row),
                   pl.BlockSpec((None, D_ATTN, tm), tmap),
                   pl.BlockSpec((tm, D_POOL), row)],
        out_shape=[jax.ShapeDtypeStruct((B, D_ATTN, S), BF16),
                   jax.ShapeDtypeStruct((T, D_ATTN), BF16),
                   jax.ShapeDtypeStruct((B, D_ATTN, S), BF16),
                   jax.ShapeDtypeStruct((T, D_POOL), F32)],
        compiler_params=_params(("arbitrary",)),
        name="inproj",
    )(x2, g, wqv_t, wk, wu)


def _attn_kernel(qa_ref, qb_ref, k_ref, vt_ref, lq1_ref, lk1_ref, lq2_ref, lk2_ref, sg_ref,
                 o_ref, qs_s, vt_s, m_s, acc_s, s_s, *, lambda_init, n_q):
    tq = qa_ref.shape[1]
    tk = tq
    dv = DIFF_V_DIM
    p = pl.program_id(2)

    @pl.when(p == 0)
    def _():
        for c in range(n_q):
            vt_s[c, 0:dv, :] = vt_ref[:, c * tk:(c + 1) * tk]
            vt_s[c, dv:, :] = jnp.ones((vt_s.shape[1] - dv, tk), BF16)

    feat = lax.broadcasted_iota(jnp.int32, (dv, tq), 0)
    for blk, q_ref in enumerate((qa_ref, qb_ref)):
        qt = q_ref[...]
        zero = jnp.zeros_like(qt)
        qs_s[blk] = jnp.concatenate([jnp.where(feat < DIFF_QK_DIM, qt, zero),
                                     jnp.where(feat >= DIFF_QK_DIM, qt, zero)], axis=1)

    key = lax.broadcasted_iota(jnp.int32, (tk, tq), 0)
    qry = lax.broadcasted_iota(jnp.int32, (tk, tq), 1)
    bias = jnp.where(key <= qry, 0.0, -jnp.inf).astype(F32)
    bias = jnp.concatenate([bias, bias], axis=1)

    items = [(0, p, True), (1, n_q - 1 - p, True)]
    for n in range(n_q - 1):
        in_a = n < p
        items.append((jnp.where(in_a, 0, 1), jnp.where(in_a, n, n - p), False))

    def scores(blk, chunk, diag):
        kc = k_ref[pl.ds(pl.multiple_of(chunk * tk, tk), tk), :]
        s = jnp.dot(kc, qs_s[blk], preferred_element_type=F32)
        return s + bias if diag else s

    for t, (blk, chunk, diag) in enumerate(items):
        s = scores(blk, chunk, diag)
        s_s[t] = s
        cmax = jnp.max(s, axis=0, keepdims=True)
        m_s[blk] = cmax if t < 2 else jnp.maximum(m_s[blk], cmax)

    for t, (blk, chunk, diag) in enumerate(items):
        pt = jnp.exp2(s_s[t] - m_s[blk]).astype(BF16)
        pv = jnp.dot(vt_s[chunk], pt, preferred_element_type=F32)
        if t < 2:
            acc_s[blk] = pv
        else:
            acc_s[blk] += pv

    lam = (jnp.exp(jnp.sum(lq1_ref[...] * lk1_ref[...]))
           - jnp.exp(jnp.sum(lq2_ref[...] * lk2_ref[...])) + lambda_init)
    for blk, qblock in enumerate((p, n_q - 1 - p)):
        acc = acc_s[blk]
        inv_l = 1.0 / acc[dv:dv + 1, :]
        o0 = acc[0:dv, 0:tq] * inv_l[:, 0:tq]
        o1 = acc[0:dv, tq:] * inv_l[:, tq:]
        a = o0 - lam * o1
        ms = jnp.mean(a * a, axis=0, keepdims=True)
        y = (a * lax.rsqrt(ms + RMS_EPS)).T * sg_ref[...]
        o_ref[pl.ds(pl.multiple_of(qblock * tq, tq), tq), :] = (y * (1.0 - lambda_init)).astype(o_ref.dtype)


def _attention(qt, k, vt, lq1, lk1, lq2, lk2, subln_g, lambda_init):
    B, _, S = qt.shape
    tq = ATTN_TILE
    n_q = S // tq
    assert S % tq == 0 and n_q % 2 == 0
    ones_rows = 16
    vec = lambda n: pl.BlockSpec((1, n), lambda b, h, p: (0, 0))
    return pl.pallas_call(
        functools.partial(_attn_kernel, lambda_init=lambda_init, n_q=n_q),
        grid=(B, N_DIFF_HEADS, n_q // 2),
        in_specs=[pl.BlockSpec((None, DIFF_V_DIM, tq), lambda b, h, p: (b, h, p)),
                  pl.BlockSpec((None, DIFF_V_DIM, tq), lambda b, h, p: (b, h, n_q - 1 - p)),
                  pl.BlockSpec((None, S, DIFF_V_DIM), lambda b, h, p: (b, 0, h)),
                  pl.BlockSpec((None, DIFF_V_DIM, S), lambda b, h, p: (b, h, 0)),
                  vec(DIFF_QK_DIM), vec(DIFF_QK_DIM), vec(DIFF_QK_DIM), vec(DIFF_QK_DIM),
                  vec(DIFF_V_DIM)],
        out_specs=pl.BlockSpec((None, S, DIFF_V_DIM), lambda b, h, p: (b, 0, h)),
        out_shape=jax.ShapeDtypeStruct((B, S, D_ATTN), BF16),
        scratch_shapes=[pltpu.VMEM((2, DIFF_V_DIM, 2 * tq), BF16),
                        pltpu.VMEM((n_q, DIFF_V_DIM + ones_rows, tq), BF16),
                        pltpu.VMEM((2, 1, 2 * tq), F32),
                        pltpu.VMEM((2, DIFF_V_DIM + ones_rows, 2 * tq), F32),
                        pltpu.VMEM((n_q + 1, tq, 2 * tq), F32)],
        compiler_params=_params(("arbitrary", "arbitrary", "arbitrary")),
        name="diff_attn",
    )(qt, qt, k, vt, lq1, lk1, lq2, lk2, subln_g)


def _pool_kernel(u_ref, w_ref, b_ref, sc_ref, o_ref):
    S = u_ref.shape[0]
    row = lax.broadcasted_iota(jnp.int32, (S, POOL_GROUP_DIM), 0)
    for g, w in enumerate(POOL_WINDOWS):
        cols = slice(g * POOL_GROUP_DIM, (g + 1) * POOL_GROUP_DIM)
        ug = u_ref[:, cols]
        s = ug
        span = 1
        while span < w:
            shifted = pltpu.roll(s, shift=span, axis=0)
            s = s + jnp.where(row >= span, shifted, 0.0)
            span *= 2
        cnt = jnp.minimum(row + 1, w).astype(F32)
        z = (s / cnt - ug).astype(BF16)
        y = jnp.dot(z, w_ref[g], preferred_element_type=F32) + b_ref[:, cols]
        o_ref[:, cols] = (y * sc_ref[:, cols]).astype(o_ref.dtype)


def _pool(u, w_pool, b_pool, pool_scale):
    B, S, _ = u.shape
    blk = pl.BlockSpec((None, S, D_POOL), lambda b: (b, 0, 0))
    return pl.pallas_call(
        _pool_kernel,
        grid=(B,),
        in_specs=[blk,
                  pl.BlockSpec((len(POOL_WINDOWS), POOL_GROUP_DIM, POOL_GROUP_DIM), lambda b: (0, 0, 0)),
                  pl.BlockSpec((1, D_POOL), lambda b: (0, 0)),
                  pl.BlockSpec((1, D_POOL), lambda b: (0, 0))],
        out_specs=blk,
        out_shape=jax.ShapeDtypeStruct((B, S, D_POOL), BF16),
        compiler_params=_params(("arbitrary",)),
        name="pool_mixer",
    )(u, w_pool, b_pool, pool_scale)


def _outproj_kernel(a_ref, p_ref, x_ref, w_ref, g_ref, rw_ref, rb_ref, x1_ref, h_ref, lg_ref):
    x1 = (x_ref[...]
          + jnp.dot(a_ref[...], w_ref[0:D_ATTN, :], preferred_element_type=F32)
          + jnp.dot(p_ref[...], w_ref[D_ATTN:, :], preferred_element_type=F32))
    x1_ref[...] = x1
    ms = jnp.mean(x1 * x1, axis=-1, keepdims=True)
    h = x1 * lax.rsqrt(ms + RMS_EPS) * g_ref[...]
    _store_token_tiles(h_ref, h)
    h_hi = h.astype(BF16)
    h_lo = (h - h_hi.astype(F32)).astype(BF16)
    both = jnp.dot(h_hi, rw_ref[...], preferred_element_type=F32)
    lg = (both[:, :LANES] + both[:, LANES:]
          + jnp.dot(h_lo, rw_ref[:, :LANES], preferred_element_type=F32) + rb_ref[...])
    lg_ref[...] = lg.T[:N_EXPERTS]


def _outproj(attn, pool, x2, w_out, g2, rw, rb):
    T = x2.shape[0]
    tm = TOKEN_TILE
    row = lambda i: (i, 0)
    fixed = lambda i: (0, 0)
    pad = ((0, 0), (0, LANES - N_EXPERTS))
    rw_hi = rw.astype(BF16)
    rw_lo = (rw - rw_hi.astype(F32)).astype(BF16)
    rw = jnp.concatenate([jnp.pad(rw_hi, pad), jnp.pad(rw_lo, pad)], axis=1)
    rb = jnp.pad(rb, pad)
    return pl.pallas_call(
        _outproj_kernel,
        grid=(T // tm,),
        in_specs=[pl.BlockSpec((tm, D_ATTN), row),
                  pl.BlockSpec((tm, D_POOL), row),
                  pl.BlockSpec((tm, D_MODEL), row),
                  pl.BlockSpec((D_MODEL, D_MODEL), fixed),
                  pl.BlockSpec((1, D_MODEL), fixed),
                  pl.BlockSpec((D_MODEL, 2 * LANES), fixed),
                  pl.BlockSpec((1, LANES), fixed)],
        out_specs=[pl.BlockSpec((tm, D_MODEL), row),
                   pl.BlockSpec((tm * TILE_ROWS, LANES), row),
                   pl.BlockSpec((None, N_EXPERTS, tm), lambda i: (i, 0, 0))],
        out_shape=[jax.ShapeDtypeStruct((T, D_MODEL), F32),
                   jax.ShapeDtypeStruct((T * TILE_ROWS, LANES), F32),
                   jax.ShapeDtypeStruct((T // tm, N_EXPERTS, tm), F32)],
        compiler_params=_params(("arbitrary",)),
        name="outproj_router",
    )(attn, pool, x2, w_out, g2, rw, rb)


def _route_kernel(lt_ref, pos_ref, gate_ref, cnt_ref, idx_s, rank_s):
    nt, E, W = lt_ref.shape
    e_iota = lax.broadcasted_iota(jnp.int32, (E, W), 0)
    before = (lax.broadcasted_iota(jnp.int32, (W, W), 0)
              < lax.broadcasted_iota(jnp.int32, (W, W), 1)).astype(BF16)
    ones = jnp.ones((W, LANES), BF16)
    widen = lambda a: jnp.concatenate([a] * (W // LANES), axis=1)

    def phase1(i, running):
        v = lt_ref[i]
        sel = jnp.zeros((E, W), F32)
        tops, hots = [], []
        for k in range(TOP_K):
            m = jnp.max(v, axis=0, keepdims=True)
            idx = jnp.min(jnp.where(v == m, e_iota, E), axis=0, keepdims=True)
            hot = e_iota == idx
            v = jnp.where(hot, -jnp.inf, v)
            sel = sel + hot.astype(F32)
            idx_s[i, k:k + 1, :] = idx
            tops.append(m)
            hots.append(hot)
        selb = sel.astype(BF16)
        rank = jnp.dot(selb, before, preferred_element_type=F32) + widen(running)
        for k in range(TOP_K):
            rank_s[i, k:k + 1, :] = jnp.sum(jnp.where(hots[k], rank, 0.0), axis=0, keepdims=True)
        ex = [jnp.exp(t - tops[0]) for t in tops]
        den = ex[0] + ex[1] + ex[2] + ex[3]
        for k in range(TOP_K):
            gate_ref[i, k:k + 1, :] = ex[k] / den
        return running + jnp.dot(selb, ones, preferred_element_type=F32)

    counts = lax.fori_loop(0, nt, phase1, jnp.zeros((E, LANES), F32))
    cnt_ref[...] = counts.astype(jnp.int32)

    row = lax.broadcasted_iota(jnp.int32, (E, LANES), 0)
    incl = counts
    span = 1
    while span < E:
        incl = incl + jnp.where(row >= span, pltpu.roll(incl, shift=span, axis=0), 0.0)
        span *= 2
    start = widen(incl - counts)

    def phase2(i, c):
        for k in range(TOP_K):
            hot = e_iota == idx_s[i, k:k + 1, :]
            base = jnp.sum(jnp.where(hot, start, 0.0), axis=0, keepdims=True)
            pos_ref[i, k:k + 1, :] = (base + rank_s[i, k:k + 1, :]).astype(jnp.int32)
        return c

    lax.fori_loop(0, nt, phase2, 0)


def _route(logits_t):
    nt, E, W = logits_t.shape
    whole = lambda shape: pl.BlockSpec(shape, lambda: (0,) * len(shape))
    return pl.pallas_call(
        _route_kernel,
        in_specs=[whole((nt, E, W))],
        out_specs=[whole((nt, TOP_K, W)), whole((nt, TOP_K, W)), whole((E, LANES))],
        out_shape=[jax.ShapeDtypeStruct((nt, TOP_K, W), jnp.int32),
                   jax.ShapeDtypeStruct((nt, TOP_K, W), F32),
                   jax.ShapeDtypeStruct((E, LANES), jnp.int32)],
        scratch_shapes=[pltpu.VMEM((nt, TOP_K, W), jnp.int32),
                        pltpu.VMEM((nt, TOP_K, W), F32)],
        compiler_params=pltpu.CompilerParams(vmem_limit_bytes=VMEM_LIMIT),
        name="route",
    )(logits_t)


def _work_items(counts):
    tm = MOE_TILE
    ends = jnp.cumsum(counts)
    starts = ends - counts
    first_blk = starts // tm
    last_blk = jnp.where(counts > 0, (ends - 1) // tm, first_blk - 1)
    n_items = last_blk - first_blk + 1
    item_end = jnp.cumsum(n_items)
    item_start = item_end - n_items
    return starts, ends, first_blk, item_start, item_end


def _item_table(counts, n_blocks):
    tm = MOE_TILE
    starts, ends, first_blk, item_start, item_end = _work_items(counts)
    n_slots = n_blocks + N_EXPERTS - 1
    j = jnp.arange(n_slots, dtype=jnp.int32)
    e = jnp.minimum(jnp.sum(item_end[None, :] <= j[:, None], axis=1), N_EXPERTS - 1).astype(jnp.int32)
    real = j < item_end[-1]
    blk = jnp.where(real, first_blk[e] + j - item_start[e], n_blocks - 1)
    lo = jnp.where(real, jnp.maximum(starts[e], blk * tm) - blk * tm, 0)
    hi = jnp.where(real, jnp.minimum(ends[e], (blk + 1) * tm) - blk * tm, 0)
    last_e = e[jnp.maximum(item_end[-1] - 1, 0)]
    e = jnp.where(real, e, last_e)
    ids = jnp.arange(N_EXPERTS, dtype=jnp.int32)
    used = counts > 0
    order = jnp.cumsum(used.astype(jnp.int32)) - 1
    later = jnp.where(jnp.logical_and(used[None, :], ids[None, :] > ids[:, None]), ids[None, :], N_EXPERTS)
    nxt_of = jnp.min(later, axis=1)
    nxt_of = jnp.where(nxt_of == N_EXPERTS, -1, nxt_of)
    as_i32 = lambda a: a.astype(jnp.int32)
    return as_i32(blk), as_i32(e), as_i32(lo), as_i32(hi), as_i32(order[e] % 2), as_i32(nxt_of[e])


def _scatter_kernel(pos_ref, src_ref, dst_ref, sem):
    tg = src_ref.shape[0]
    for r in range(tg):
        for k in range(TOP_K):
            pltpu.make_async_copy(src_ref.at[r], dst_ref.at[pos_ref[0, 0, k * tg + r]],
                                  sem).start(priority=(r * TOP_K + k) % 2)
    for k in range(TOP_K):
        pltpu.make_async_copy(src_ref, dst_ref.at[pl.ds(0, tg)], sem).wait()


def _scatter_rows(src, pos_tiles):
    T = src.shape[0]
    tg = GATHER_TILE
    return pl.pallas_call(
        _scatter_kernel,
        grid=(T // tg,),
        in_specs=[pl.BlockSpec((1, 1, tg * TOP_K), lambda i: (i, 0, 0), memory_space=pltpu.SMEM),
                  pl.BlockSpec((tg, TILE_ROWS, LANES), lambda i: (i, 0, 0))],
        out_specs=pl.BlockSpec(memory_space=pl.ANY),
        out_shape=jax.ShapeDtypeStruct((T * TOP_K, TILE_ROWS, LANES), src.dtype),
        scratch_shapes=[pltpu.SemaphoreType.DMA(())],
        compiler_params=_params(("arbitrary",)),
        name="row_scatter",
    )(pos_tiles, src)


def _moe_kernel(blk_ref, e_ref, lo_ref, hi_ref, par_ref, nxt_ref, xs_ref, wg_hbm, bg_ref, wu_hbm,
                bu_ref, wd_hbm, bd_ref, y_ref, wg_s, wu_s, wd_s, wbuf, wsem):
    j = pl.program_id(0)
    jp = jnp.maximum(j - 1, 0)
    tm = MOE_TILE
    srcs = (wg_hbm, wu_hbm, wd_hbm)

    def weight_copies(expert, buffer):
        return [pltpu.make_async_copy(src.at[expert], wbuf.at[buffer, m], wsem.at[buffer])
                for m, src in enumerate(srcs)]

    @pl.when(j == 0)
    def _():
        for cp in weight_copies(e_ref[0], par_ref[0]):
            cp.start()

    @pl.when(jnp.logical_or(j == 0, e_ref[j] != e_ref[jp]))
    def _():
        buffer = par_ref[j]
        for cp in weight_copies(e_ref[j], buffer):
            cp.wait()
        rows = 128
        for c in range(D_MODEL // rows):
            sl = slice(c * rows, (c + 1) * rows)
            wg_s[sl, :] = wbuf[buffer, 0, sl, :].astype(BF16)
            wu_s[sl, :] = wbuf[buffer, 1, sl, :].astype(BF16)
            wd_s[sl, :] = wbuf[buffer, 2, sl, :].astype(BF16)

        @pl.when(nxt_ref[j] >= 0)
        def _():
            for cp in weight_copies(nxt_ref[j], 1 - buffer):
                cp.start()

    lo = lo_ref[j]
    hi = hi_ref[j]

    first_visit = jnp.logical_or(j == 0, blk_ref[j] != blk_ref[jp])
    half = tm // 2

    def tile_rows(r0, n, c):
        return pl.ds(r0 * TILE_ROWS + c, n, stride=TILE_ROWS)

    def mlp(r0, n):
        x = jnp.concatenate([xs_ref[tile_rows(r0, n, c), :].astype(BF16) for c in range(TILE_ROWS)],
                            axis=1)
        y = None
        for f in range(D_FF // FF_CHUNK):
            cols = slice(f * FF_CHUNK, (f + 1) * FF_CHUNK)
            g = jnp.dot(x, wg_s[:, cols], preferred_element_type=F32) + bg_ref[:, cols]
            g = jnp.minimum(g, SWIGLU_LIMIT)
            u = jnp.dot(x, wu_s[:, cols], preferred_element_type=F32) + bu_ref[:, cols]
            u = jnp.clip(u, -SWIGLU_LIMIT, SWIGLU_LIMIT)
            act = (g * jax.nn.sigmoid(SWIGLU_ALPHA * g) * (u + 1.0)).astype(BF16)
            part = jnp.dot(act, wd_s[cols, :], preferred_element_type=F32)
            y = part if y is None else y + part
        return y + bd_ref[...]

    @pl.when(jnp.logical_and(lo == 0, hi == tm))
    def _():
        y = mlp(0, tm)
        for c in range(TILE_ROWS):
            y_ref[tile_rows(0, tm, c), :] = y[:, c * LANES:(c + 1) * LANES]

    def masked_rows(r0, n):
        y = mlp(r0, n)
        rows = r0 + lax.broadcasted_iota(jnp.int32, (n, LANES), 0)
        mine = jnp.logical_and(rows >= lo, rows < hi)

        @pl.when(first_visit)
        def _():
            for c in range(TILE_ROWS):
                y_ref[tile_rows(r0, n, c), :] = jnp.where(mine, y[:, c * LANES:(c + 1) * LANES], 0.0)

        @pl.when(jnp.logical_not(first_visit))
        def _():
            for c in range(TILE_ROWS):
                sl = tile_rows(r0, n, c)
                y_ref[sl, :] = jnp.where(mine, y[:, c * LANES:(c + 1) * LANES], y_ref[sl, :])

    partial = jnp.logical_and(lo < hi, jnp.logical_or(lo > 0, hi < tm))
    spans_both = jnp.logical_and(lo < half, hi > half)

    @pl.when(jnp.logical_and(partial, spans_both))
    def _():
        masked_rows(0, tm)

    for r0 in (0, half):
        inside = jnp.logical_and(lo >= r0, hi <= r0 + half)

        @pl.when(jnp.logical_and(partial, inside))
        def _():
            masked_rows(r0, half)

            @pl.when(first_visit)
            def _():
                other = half - r0
                y_ref[pl.ds(other * TILE_ROWS, half * TILE_ROWS), :] = jnp.zeros(
                    (half * TILE_ROWS, LANES), y_ref.dtype)


def _moe_experts(items, xs, wg, bg, wu, bu, wd, bd):
    P = xs.shape[0] // TILE_ROWS
    tm = MOE_TILE
    n_items = items[0].shape[0]
    assert D_FF == D_MODEL
    tiles = pl.BlockSpec((tm * TILE_ROWS, LANES), lambda j, blk, e, lo, hi, par, nxt: (blk[j], 0))
    wspec = pl.BlockSpec(memory_space=pl.ANY)
    bspec = pl.BlockSpec((None, 1, D_FF), lambda j, blk, e, lo, hi, par, nxt: (e[j], 0, 0))
    grid_spec = pltpu.PrefetchScalarGridSpec(
        num_scalar_prefetch=6,
        grid=(n_items,),
        in_specs=[tiles, wspec, bspec, wspec, bspec, wspec, bspec],
        out_specs=tiles,
        scratch_shapes=[pltpu.VMEM((D_MODEL, D_FF), BF16),
                        pltpu.VMEM((D_MODEL, D_FF), BF16),
                        pltpu.VMEM((D_FF, D_MODEL), BF16),
                        pltpu.VMEM((2, 3, D_MODEL, D_FF), F32),
                        pltpu.SemaphoreType.DMA((2,))],
    )
    return pl.pallas_call(
        _moe_kernel,
        grid_spec=grid_spec,
        out_shape=jax.ShapeDtypeStruct((P * TILE_ROWS, LANES), F32),
        compiler_params=_params(("arbitrary",)),
        name="moe_experts",
    )(*items, xs, wg, bg, wu, bu, wd, bd)


def _combine_kernel(pos_ref, x1_ref, gt_ref, g_ref, ys_ref, o_ref, buf, sem):
    i = pl.program_id(0)
    n_tiles = pl.num_programs(0) - 1
    tm = x1_ref.shape[0]
    n_rows = TOP_K * tm

    @pl.when(i < n_tiles)
    def _():
        slot = i % 2
        for n in range(n_rows):
            src = ys_ref.at[pl.ds(pl.multiple_of(pos_ref[0, 0, n] * TILE_ROWS, TILE_ROWS), TILE_ROWS), :]
            pltpu.make_async_copy(src, buf.at[slot, pl.ds(n * TILE_ROWS, TILE_ROWS), :],
                                  sem.at[slot]).start(priority=n % 2)

    @pl.when(i > 0)
    def _():
        slot = (i - 1) % 2
        pltpu.make_async_copy(ys_ref.at[pl.ds(0, n_rows * TILE_ROWS), :], buf.at[slot], sem.at[slot]).wait()
        x = x1_ref[...]
        gates = gt_ref[...]
        for k in range(TOP_K):
            x = x + gates[:, k:k + 1] * _load_token_tiles(
                buf.at[slot, pl.ds(k * tm * TILE_ROWS, tm * TILE_ROWS), :], tm)
        ms = jnp.mean(x * x, axis=-1, keepdims=True)
        o_ref[...] = x * lax.rsqrt(ms + RMS_EPS) * g_ref[...]


def _combine(x1, ys, pos_tiles, gates, g):
    T = x1.shape[0]
    tm = COMBINE_TILE
    n_tiles = T // tm
    done = lambda i: (jnp.maximum(i - 1, 0), 0)
    return pl.pallas_call(
        _combine_kernel,
        grid=(n_tiles + 1,),
        in_specs=[pl.BlockSpec((1, 1, TOP_K * tm), lambda i: (jnp.minimum(i, n_tiles - 1), 0, 0),
                               memory_space=pltpu.SMEM),
                  pl.BlockSpec((tm, D_MODEL), done),
                  pl.BlockSpec((tm, TOP_K), done),
                  pl.BlockSpec((1, D_MODEL), lambda i: (0, 0)),
                  pl.BlockSpec(memory_space=pl.ANY)],
        out_specs=pl.BlockSpec((tm, D_MODEL), done),
        out_shape=jax.ShapeDtypeStruct((T, D_MODEL), F32),
        scratch_shapes=[pltpu.VMEM((2, TOP_K * tm * TILE_ROWS, LANES), F32),
                        pltpu.SemaphoreType.DMA((2,))],
        compiler_params=_params(("arbitrary",)),
        name="gather_combine_norm",
    )(pos_tiles, x1, gates, g, ys)


def kernel(x, norm1_g, w_in, lambda_q1, lambda_k1, lambda_q2, lambda_k2, subln_g, w_pool, b_pool,
           pool_scale, w_out, norm2_g, router_w, router_b, w_gate, b_gate, w_up, b_up, w_down,
           b_down, final_g):
    B, S, D = x.shape
    T = B * S
    l = 0
    lambda_init = 0.8 - 0.6 * math.exp(-0.3 * l)
    x2 = x.reshape(T, D)

    w = w_in[l]
    wqv_t = jnp.concatenate([w[:, :D_ATTN], w[:, 2 * D_ATTN:3 * D_ATTN]], axis=1).T.astype(BF16)
    qt, k, vt, u = _inproj(x2, norm1_g[l][None, :], wqv_t, w[:, D_ATTN:2 * D_ATTN].astype(BF16),
                           w[:, 3 * D_ATTN:].astype(BF16), B, S)
    attn = _attention(qt, k.reshape(B, S, D_ATTN), vt,
                      lambda_q1[l][None, :], lambda_k1[l][None, :],
                      lambda_q2[l][None, :], lambda_k2[l][None, :],
                      subln_g[l][None, :], lambda_init)
    pool = _pool(u.reshape(B, S, D_POOL), w_pool[l].astype(BF16),
                 b_pool[l].reshape(1, D_POOL), pool_scale[l][None, :])
    x1, h2, logits_t = _outproj(attn.reshape(T, D_ATTN), pool.reshape(T, D_POOL), x2,
                                w_out[l].astype(BF16), norm2_g[l][None, :],
                                router_w[l], router_b[l][None, :])

    pos_t, gates_t, counts = _route(logits_t)
    nt, _, W = pos_t.shape
    assert GATHER_TILE == COMBINE_TILE
    pos_tiles = (pos_t.reshape(nt, TOP_K, W // COMBINE_TILE, COMBINE_TILE).transpose(0, 2, 1, 3)
                 .reshape(T // COMBINE_TILE, 1, TOP_K * COMBINE_TILE))
    gates = gates_t.transpose(0, 2, 1).reshape(T, TOP_K)
    assert (T * TOP_K) % MOE_TILE == 0
    items = _item_table(counts[:, 0], T * TOP_K // MOE_TILE)

    xs = _scatter_rows(h2.reshape(T, TILE_ROWS, LANES), pos_tiles)
    ys = _moe_experts(items, xs.reshape(-1, LANES),
                      w_gate[l], b_gate[l][:, None, :], w_up[l], b_up[l][:, None, :],
                      w_down[l], b_down[l][:, None, :])
    out = _combine(x1, ys, pos_tiles, gates, final_g[None, :])
    return out.reshape(B, S, D)
```

```python
import functools
import math

import jax
import jax.numpy as jnp
from jax import lax
from jax.experimental import pallas as pl
from jax.experimental.pallas import tpu as pltpu

D_MODEL = 1024
D_ATTN = 512
D_POOL = 512
N_DIFF_HEADS = 4
DIFF_QK_DIM = 64
DIFF_V_DIM = 128
POOL_WINDOWS = (2, 4, 8, 16)
POOL_GROUP_DIM = 128
N_EXPERTS = 32
TOP_K = 4
D_FF = 1024
SWIGLU_LIMIT = 7.0
SWIGLU_ALPHA = 1.702
RMS_EPS = 1e-5

F32 = jnp.float32
BF16 = jnp.bfloat16

TOKEN_TILE = 512
ATTN_TILE = 256
MOE_TILE = 512
GATHER_TILE = 512
COMBINE_TILE = 512
FF_CHUNK = 512
VMEM_LIMIT = 56 * 1024 * 1024


def _params(sem, vmem=VMEM_LIMIT):
    return pltpu.CompilerParams(dimension_semantics=sem, vmem_limit_bytes=vmem)


LANES = 128
TILE_ROWS = D_MODEL // LANES
assert TILE_ROWS == 8


def _store_token_tiles(ref, x):
    n = x.shape[0]
    for c in range(TILE_ROWS):
        ref[pl.ds(c, n, stride=TILE_ROWS), :] = x[:, c * LANES:(c + 1) * LANES]


def _load_token_tiles(ref, n, dtype=None):
    cols = [ref[pl.ds(c, n, stride=TILE_ROWS), :] for c in range(TILE_ROWS)]
    if dtype is not None:
        cols = [c.astype(dtype) for c in cols]
    return jnp.concatenate(cols, axis=1)


def _inproj_kernel(x_ref, g_ref, wqv_ref, wk_ref, wu_ref, qt_ref, k_ref, vt_ref, u_ref):
    x = x_ref[...]
    ms = jnp.mean(x * x, axis=-1, keepdims=True)
    h = (x * lax.rsqrt(ms + RMS_EPS) * g_ref[...]).astype(BF16)
    qvt = lax.dot_general(wqv_ref[...], h, (((1,), (1,)), ((), ())), preferred_element_type=F32)
    qt_ref[...] = (qvt[:D_ATTN] * (DIFF_QK_DIM ** -0.5 * math.log2(math.e))).astype(BF16)
    vt_ref[...] = qvt[D_ATTN:].astype(BF16)
    k_ref[...] = jnp.dot(h, wk_ref[...], preferred_element_type=F32).astype(BF16)
    u_ref[...] = jnp.dot(h, wu_ref[...], preferred_element_type=F32)


def _inproj(x2, g, wqv_t, wk, wu, B, S):
    T = x2.shape[0]
    tm = TOKEN_TILE
    per_seq = S // tm
    row = lambda i: (i, 0)
    fixed = lambda i: (0, 0)
    tmap = lambda i: (i // per_seq, 0, i % per_seq)
    return pl.pallas_call(
        _inproj_kernel,
        grid=(T // tm,),
        in_specs=[pl.BlockSpec((tm, D_MODEL), row),
                  pl.BlockSpec((1, D_MODEL), fixed),
                  pl.BlockSpec((2 * D_ATTN, D_MODEL), fixed),
                  pl.BlockSpec((D_MODEL, D_ATTN), fixed),
                  pl.BlockSpec((D_MODEL, D_POOL), fixed)],
        out_specs=[pl.BlockSpec((None, D_ATTN, tm), tmap),
                   pl.BlockSpec((tm, D_ATTN), row),
                   pl.BlockSpec((None, D_ATTN, tm), tmap),
                   pl.BlockSpec((tm, D_POOL), row)],
        out_shape=[jax.ShapeDtypeStruct((B, D_ATTN, S), BF16),
                   jax.ShapeDtypeStruct((T, D_ATTN), BF16),
                   jax.ShapeDtypeStruct((B, D_ATTN, S), BF16),
                   jax.ShapeDtypeStruct((T, D_POOL), F32)],
        compiler_params=_params(("arbitrary",)),
        name="inproj",
    )(x2, g, wqv_t, wk, wu)


def _attn_kernel(qa_ref, qb_ref, k_ref, vt_ref, lq1_ref, lk1_ref, lq2_ref, lk2_ref, sg_ref,
                 o_ref, qs_s, vt_s, m_s, acc_s, s_s, *, lambda_init, n_q):
    tq = qa_ref.shape[1]
    tk = tq
    dv = DIFF_V_DIM
    p = pl.program_id(2)

    @pl.when(p == 0)
    def _():
        for c in range(n_q):
            vt_s[c, 0:dv, :] = vt_ref[:, c * tk:(c + 1) * tk]
            vt_s[c, dv:, :] = jnp.ones((vt_s.shape[1] - dv, tk), BF16)

    feat = lax.broadcasted_iota(jnp.int32, (dv, tq), 0)
    for blk, q_ref in enumerate((qa_ref, qb_ref)):
        qt = q_ref[...]
        zero = jnp.zeros_like(qt)
        qs_s[blk] = jnp.concatenate([jnp.where(feat < DIFF_QK_DIM, qt, zero),
                                     jnp.where(feat >= DIFF_QK_DIM, qt, zero)], axis=1)
    m_s[...] = jnp.full(m_s.shape, -jnp.inf, F32)
    acc_s[...] = jnp.zeros(acc_s.shape, F32)

    key = lax.broadcasted_iota(jnp.int32, (tk, tq), 0)
    qry = lax.broadcasted_iota(jnp.int32, (tk, tq), 1)
    bias = jnp.where(key <= qry, 0.0, -jnp.inf).astype(F32)
    bias = jnp.concatenate([bias, bias], axis=1)

    items = [(0, p, True), (1, n_q - 1 - p, True)]
    for n in range(n_q - 1):
        in_a = n < p
        items.append((jnp.where(in_a, 0, 1), jnp.where(in_a, n, n - p), False))

    def scores(blk, chunk, diag):
        kc = k_ref[pl.ds(pl.multiple_of(chunk * tk, tk), tk), :]
        s = jnp.dot(kc, qs_s[blk], preferred_element_type=F32)
        return s + bias if diag else s

    for t, (blk, chunk, diag) in enumerate(items):
        s = scores(blk, chunk, diag)
        s_s[t] = s
        m_s[blk] = jnp.maximum(m_s[blk], jnp.max(s, axis=0, keepdims=True))

    for t, (blk, chunk, diag) in enumerate(items):
        pt = jnp.exp2(s_s[t] - m_s[blk]).astype(BF16)
        acc_s[blk] += jnp.dot(vt_s[chunk], pt, preferred_element_type=F32)

    lam = (jnp.exp(jnp.sum(lq1_ref[...] * lk1_ref[...]))
           - jnp.exp(jnp.sum(lq2_ref[...] * lk2_ref[...])) + lambda_init)
    for blk, qblock in enumerate((p, n_q - 1 - p)):
        acc = acc_s[blk]
        o0 = acc[0:dv, 0:tq] / acc[dv:dv + 1, 0:tq]
        o1 = acc[0:dv, tq:] / acc[dv:dv + 1, tq:]
        a = o0 - lam * o1
        ms = jnp.mean(a * a, axis=0, keepdims=True)
        y = (a * lax.rsqrt(ms + RMS_EPS)).T * sg_ref[...]
        o_ref[pl.ds(pl.multiple_of(qblock * tq, tq), tq), :] = (y * (1.0 - lambda_init)).astype(o_ref.dtype)


def _attention(qt, k, vt, lq1, lk1, lq2, lk2, subln_g, lambda_init):
    B, _, S = qt.shape
    tq = ATTN_TILE
    n_q = S // tq
    assert S % tq == 0 and n_q % 2 == 0
    ones_rows = 16
    vec = lambda n: pl.BlockSpec((1, n), lambda b, h, p: (0, 0))
    return pl.pallas_call(
        functools.partial(_attn_kernel, lambda_init=lambda_init, n_q=n_q),
        grid=(B, N_DIFF_HEADS, n_q // 2),
        in_specs=[pl.BlockSpec((None, DIFF_V_DIM, tq), lambda b, h, p: (b, h, p)),
                  pl.BlockSpec((None, DIFF_V_DIM, tq), lambda b, h, p: (b, h, n_q - 1 - p)),
                  pl.BlockSpec((None, S, DIFF_V_DIM), lambda b, h, p: (b, 0, h)),
                  pl.BlockSpec((None, DIFF_V_DIM, S), lambda b, h, p: (b, h, 0)),
                  vec(DIFF_QK_DIM), vec(DIFF_QK_DIM), vec(DIFF_QK_DIM), vec(DIFF_QK_DIM),
                  vec(DIFF_V_DIM)],
        out_specs=pl.BlockSpec((None, S, DIFF_V_DIM), lambda b, h, p: (b, 0, h)),
        out_shape=jax.ShapeDtypeStruct((B, S, D_ATTN), BF16),
        scratch_shapes=[pltpu.VMEM((2, DIFF_V_DIM, 2 * tq), BF16),
                        pltpu.VMEM((n_q, DIFF_V_DIM + ones_rows, tq), BF16),
                        pltpu.VMEM((2, 1, 2 * tq), F32),
                        pltpu.VMEM((2, DIFF_V_DIM + ones_rows, 2 * tq), F32),
                        pltpu.VMEM((n_q + 1, tq, 2 * tq), F32)],
        compiler_params=_params(("arbitrary", "arbitrary", "arbitrary")),
        name="diff_attn",
    )(qt, qt, k, vt, lq1, lk1, lq2, lk2, subln_g)


def _pool_kernel(u_ref, w_ref, b_ref, sc_ref, o_ref):
    S = u_ref.shape[0]
    row = lax.broadcasted_iota(jnp.int32, (S, POOL_GROUP_DIM), 0)
    for g, w in enumerate(POOL_WINDOWS):
        cols = slice(g * POOL_GROUP_DIM, (g + 1) * POOL_GROUP_DIM)
        ug = u_ref[:, cols]
        s = ug
        span = 1
        while span < w:
            shifted = pltpu.roll(s, shift=span, axis=0)
            s = s + jnp.where(row >= span, shifted, 0.0)
            span *= 2
        cnt = jnp.minimum(row + 1, w).astype(F32)
        z = (s / cnt - ug).astype(BF16)
        y = jnp.dot(z, w_ref[g], preferred_element_type=F32) + b_ref[:, cols]
        o_ref[:, cols] = (y * sc_ref[:, cols]).astype(o_ref.dtype)


def _pool(u, w_pool, b_pool, pool_scale):
    B, S, _ = u.shape
    blk = pl.BlockSpec((None, S, D_POOL), lambda b: (b, 0, 0))
    return pl.pallas_call(
        _pool_kernel,
        grid=(B,),
        in_specs=[blk,
                  pl.BlockSpec((len(POOL_WINDOWS), POOL_GROUP_DIM, POOL_GROUP_DIM), lambda b: (0, 0, 0)),
                  pl.BlockSpec((1, D_POOL), lambda b: (0, 0)),
                  pl.BlockSpec((1, D_POOL), lambda b: (0, 0))],
        out_specs=blk,
        out_shape=jax.ShapeDtypeStruct((B, S, D_POOL), BF16),
        compiler_params=_params(("arbitrary",)),
        name="pool_mixer",
    )(u, w_pool, b_pool, pool_scale)


def _outproj_kernel(a_ref, p_ref, x_ref, w_ref, g_ref, rw_ref, rb_ref, x1_ref, h_ref, lg_ref):
    x1 = (x_ref[...]
          + jnp.dot(a_ref[...], w_ref[0:D_ATTN, :], preferred_element_type=F32)
          + jnp.dot(p_ref[...], w_ref[D_ATTN:, :], preferred_element_type=F32))
    x1_ref[...] = x1
    ms = jnp.mean(x1 * x1, axis=-1, keepdims=True)
    h = x1 * lax.rsqrt(ms + RMS_EPS) * g_ref[...]
    _store_token_tiles(h_ref, h)
    h_hi = h.astype(BF16)
    h_lo = (h - h_hi.astype(F32)).astype(BF16)
    both = jnp.dot(h_hi, rw_ref[...], preferred_element_type=F32)
    lg = (both[:, :LANES] + both[:, LANES:]
          + jnp.dot(h_lo, rw_ref[:, :LANES], preferred_element_type=F32) + rb_ref[...])
    lg_ref[...] = lg.T[:N_EXPERTS]


def _outproj(attn, pool, x2, w_out, g2, rw, rb):
    T = x2.shape[0]
    tm = TOKEN_TILE
    row = lambda i: (i, 0)
    fixed = lambda i: (0, 0)
    pad = ((0, 0), (0, LANES - N_EXPERTS))
    rw_hi = rw.astype(BF16)
    rw_lo = (rw - rw_hi.astype(F32)).astype(BF16)
    rw = jnp.concatenate([jnp.pad(rw_hi, pad), jnp.pad(rw_lo, pad)], axis=1)
    rb = jnp.pad(rb, pad)
    return pl.pallas_call(
        _outproj_kernel,
        grid=(T // tm,),
        in_specs=[pl.BlockSpec((tm, D_ATTN), row),
                  pl.BlockSpec((tm, D_POOL), row),
                  pl.BlockSpec((tm, D_MODEL), row),
                  pl.BlockSpec((D_MODEL, D_MODEL), fixed),
                  pl.BlockSpec((1, D_MODEL), fixed),
                  pl.BlockSpec((D_MODEL, 2 * LANES), fixed),
                  pl.BlockSpec((1, LANES), fixed)],
        out_specs=[pl.BlockSpec((tm, D_MODEL), row),
                   pl.BlockSpec((tm * TILE_ROWS, LANES), row),
                   pl.BlockSpec((None, N_EXPERTS, tm), lambda i: (i, 0, 0))],
        out_shape=[jax.ShapeDtypeStruct((T, D_MODEL), F32),
                   jax.ShapeDtypeStruct((T * TILE_ROWS, LANES), F32),
                   jax.ShapeDtypeStruct((T // tm, N_EXPERTS, tm), F32)],
        compiler_params=_params(("arbitrary",)),
        name="outproj_router",
    )(attn, pool, x2, w_out, g2, rw, rb)


def _route_kernel(lt_ref, pos_ref, gate_ref, cnt_ref, idx_s, rank_s):
    nt, E, W = lt_ref.shape
    e_iota = lax.broadcasted_iota(jnp.int32, (E, W), 0)
    before = (lax.broadcasted_iota(jnp.int32, (W, W), 0)
              < lax.broadcasted_iota(jnp.int32, (W, W), 1)).astype(BF16)
    ones = jnp.ones((W, LANES), BF16)
    widen = lambda a: jnp.concatenate([a] * (W // LANES), axis=1)

    def phase1(i, running):
        v = lt_ref[i]
        sel = jnp.zeros((E, W), F32)
        tops, hots = [], []
        for k in range(TOP_K):
            m = jnp.max(v, axis=0, keepdims=True)
            idx = jnp.min(jnp.where(v == m, e_iota, E), axis=0, keepdims=True)
            hot = e_iota == idx
            v = jnp.where(hot, -jnp.inf, v)
            sel = sel + hot.astype(F32)
            idx_s[i, k:k + 1, :] = idx
            tops.append(m)
            hots.append(hot)
        selb = sel.astype(BF16)
        rank = jnp.dot(selb, before, preferred_element_type=F32) + widen(running)
        for k in range(TOP_K):
            rank_s[i, k:k + 1, :] = jnp.sum(jnp.where(hots[k], rank, 0.0), axis=0, keepdims=True)
        ex = [jnp.exp(t - tops[0]) for t in tops]
        den = ex[0] + ex[1] + ex[2] + ex[3]
        for k in range(TOP_K):
            gate_ref[i, k:k + 1, :] = ex[k] / den
        return running + jnp.dot(selb, ones, preferred_element_type=F32)

    counts = lax.fori_loop(0, nt, phase1, jnp.zeros((E, LANES), F32))
    cnt_ref[...] = counts.astype(jnp.int32)

    row = lax.broadcasted_iota(jnp.int32, (E, LANES), 0)
    incl = counts
    span = 1
    while span < E:
        incl = incl + jnp.where(row >= span, pltpu.roll(incl, shift=span, axis=0), 0.0)
        span *= 2
    start = widen(incl - counts)

    def phase2(i, c):
        for k in range(TOP_K):
            hot = e_iota == idx_s[i, k:k + 1, :]
            base = jnp.sum(jnp.where(hot, start, 0.0), axis=0, keepdims=True)
            pos_ref[i, k:k + 1, :] = (base + rank_s[i, k:k + 1, :]).astype(jnp.int32)
        return c

    lax.fori_loop(0, nt, phase2, 0)


def _route(logits_t):
    nt, E, W = logits_t.shape
    whole = lambda shape: pl.BlockSpec(shape, lambda: (0,) * len(shape))
    return pl.pallas_call(
        _route_kernel,
        in_specs=[whole((nt, E, W))],
        out_specs=[whole((nt, TOP_K, W)), whole((nt, TOP_K, W)), whole((E, LANES))],
        out_shape=[jax.ShapeDtypeStruct((nt, TOP_K, W), jnp.int32),
                   jax.ShapeDtypeStruct((nt, TOP_K, W), F32),
                   jax.ShapeDtypeStruct((E, LANES), jnp.int32)],
        scratch_shapes=[pltpu.VMEM((nt, TOP_K, W), jnp.int32),
                        pltpu.VMEM((nt, TOP_K, W), F32)],
        compiler_params=pltpu.CompilerParams(vmem_limit_bytes=VMEM_LIMIT),
        name="route",
    )(logits_t)


def _work_items(counts):
    tm = MOE_TILE
    ends = jnp.cumsum(counts)
    starts = ends - counts
    first_blk = starts // tm
    last_blk = jnp.where(counts > 0, (ends - 1) // tm, first_blk - 1)
    n_items = last_blk - first_blk + 1
    item_end = jnp.cumsum(n_items)
    item_start = item_end - n_items
    return starts, ends, first_blk, item_start, item_end


def _item_table(counts, n_blocks):
    tm = MOE_TILE
    starts, ends, first_blk, item_start, item_end = _work_items(counts)
    n_slots = n_blocks + N_EXPERTS - 1
    j = jnp.arange(n_slots, dtype=jnp.int32)
    e = jnp.minimum(jnp.sum(item_end[None, :] <= j[:, None], axis=1), N_EXPERTS - 1).astype(jnp.int32)
    real = j < item_end[-1]
    blk = jnp.where(real, first_blk[e] + j - item_start[e], n_blocks - 1)
    lo = jnp.where(real, jnp.maximum(starts[e], blk * tm) - blk * tm, 0)
    hi = jnp.where(real, jnp.minimum(ends[e], (blk + 1) * tm) - blk * tm, 0)
    last_e = e[jnp.maximum(item_end[-1] - 1, 0)]
    e = jnp.where(real, e, last_e)
    ids = jnp.arange(N_EXPERTS, dtype=jnp.int32)
    used = counts > 0
    order = jnp.cumsum(used.astype(jnp.int32)) - 1
    later = jnp.where(jnp.logical_and(used[None, :], ids[None, :] > ids[:, None]), ids[None, :], N_EXPERTS)
    nxt_of = jnp.min(later, axis=1)
    nxt_of = jnp.where(nxt_of == N_EXPERTS, -1, nxt_of)
    as_i32 = lambda a: a.astype(jnp.int32)
    return as_i32(blk), as_i32(e), as_i32(lo), as_i32(hi), as_i32(order[e] % 2), as_i32(nxt_of[e])


def _scatter_kernel(pos_ref, src_ref, dst_ref, sem):
    tg = src_ref.shape[0]
    for r in range(tg):
        for k in range(TOP_K):
            pltpu.make_async_copy(src_ref.at[r], dst_ref.at[pos_ref[0, 0, k * tg + r]],
                                  sem).start(priority=(r * TOP_K + k) % 2)
    for k in range(TOP_K):
        pltpu.make_async_copy(src_ref, dst_ref.at[pl.ds(0, tg)], sem).wait()


def _scatter_rows(src, pos_tiles):
    T = src.shape[0]
    tg = GATHER_TILE
    return pl.pallas_call(
        _scatter_kernel,
        grid=(T // tg,),
        in_specs=[pl.BlockSpec((1, 1, tg * TOP_K), lambda i: (i, 0, 0), memory_space=pltpu.SMEM),
                  pl.BlockSpec((tg, TILE_ROWS, LANES), lambda i: (i, 0, 0))],
        out_specs=pl.BlockSpec(memory_space=pl.ANY),
        out_shape=jax.ShapeDtypeStruct((T * TOP_K, TILE_ROWS, LANES), src.dtype),
        scratch_shapes=[pltpu.SemaphoreType.DMA(())],
        compiler_params=_params(("arbitrary",)),
        name="row_scatter",
    )(pos_tiles, src)


def _moe_kernel(blk_ref, e_ref, lo_ref, hi_ref, par_ref, nxt_ref, xs_ref, wg_hbm, bg_ref, wu_hbm,
                bu_ref, wd_hbm, bd_ref, y_ref, wg_s, wu_s, wd_s, wbuf, wsem):
    j = pl.program_id(0)
    jp = jnp.maximum(j - 1, 0)
    tm = MOE_TILE
    srcs = (wg_hbm, wu_hbm, wd_hbm)

    def weight_copies(expert, buffer):
        return [pltpu.make_async_copy(src.at[expert], wbuf.at[buffer, m], wsem.at[buffer])
                for m, src in enumerate(srcs)]

    @pl.when(j == 0)
    def _():
        for cp in weight_copies(e_ref[0], par_ref[0]):
            cp.start()

    @pl.when(jnp.logical_or(j == 0, e_ref[j] != e_ref[jp]))
    def _():
        buffer = par_ref[j]
        for cp in weight_copies(e_ref[j], buffer):
            cp.wait()
        rows = 128
        for c in range(D_MODEL // rows):
            sl = slice(c * rows, (c + 1) * rows)
            wg_s[sl, :] = wbuf[buffer, 0, sl, :].astype(BF16)
            wu_s[sl, :] = wbuf[buffer, 1, sl, :].astype(BF16)
            wd_s[sl, :] = wbuf[buffer, 2, sl, :].astype(BF16)

        @pl.when(nxt_ref[j] >= 0)
        def _():
            for cp in weight_copies(nxt_ref[j], 1 - buffer):
                cp.start()

    lo = lo_ref[j]
    hi = hi_ref[j]

    first_visit = jnp.logical_or(j == 0, blk_ref[j] != blk_ref[jp])
    half = tm // 2

    def tile_rows(r0, n, c):
        return pl.ds(r0 * TILE_ROWS + c, n, stride=TILE_ROWS)

    def mlp(r0, n):
        x = jnp.concatenate([xs_ref[tile_rows(r0, n, c), :].astype(BF16) for c in range(TILE_ROWS)],
                            axis=1)
        y = None
        for f in range(D_FF // FF_CHUNK):
            cols = slice(f * FF_CHUNK, (f + 1) * FF_CHUNK)
            g = jnp.dot(x, wg_s[:, cols], preferred_element_type=F32) + bg_ref[:, cols]
            g = jnp.minimum(g, SWIGLU_LIMIT)
            u = jnp.dot(x, wu_s[:, cols], preferred_element_type=F32) + bu_ref[:, cols]
            u = jnp.clip(u, -SWIGLU_LIMIT, SWIGLU_LIMIT)
            act = (g * jax.nn.sigmoid(SWIGLU_ALPHA * g) * (u + 1.0)).astype(BF16)
            part = jnp.dot(act, wd_s[cols, :], preferred_element_type=F32)
            y = part if y is None else y + part
        return y + bd_ref[...]

    @pl.when(jnp.logical_and(lo == 0, hi == tm))
    def _():
        y = mlp(0, tm)
        for c in range(TILE_ROWS):
            y_ref[tile_rows(0, tm, c), :] = y[:, c * LANES:(c + 1) * LANES]

    def masked_rows(r0, n):
        y = mlp(r0, n)
        rows = r0 + lax.broadcasted_iota(jnp.int32, (n, LANES), 0)
        mine = jnp.logical_and(rows >= lo, rows < hi)

        @pl.when(first_visit)
        def _():
            for c in range(TILE_ROWS):
                y_ref[tile_rows(r0, n, c), :] = jnp.where(mine, y[:, c * LANES:(c + 1) * LANES], 0.0)

        @pl.when(jnp.logical_not(first_visit))
        def _():
            for c in range(TILE_ROWS):
                sl = tile_rows(r0, n, c)
                y_ref[sl, :] = jnp.where(mine, y[:, c * LANES:(c + 1) * LANES], y_ref[sl, :])

    partial = jnp.logical_and(lo < hi, jnp.logical_or(lo > 0, hi < tm))
    spans_both = jnp.logical_and(lo < half, hi > half)

    @pl.when(jnp.logical_and(partial, spans_both))
    def _():
        masked_rows(0, tm)

    for r0 in (0, half):
        inside = jnp.logical_and(lo >= r0, hi <= r0 + half)

        @pl.when(jnp.logical_and(partial, inside))
        def _():
            masked_rows(r0, half)

            @pl.when(first_visit)
            def _():
                other = half - r0
                y_ref[pl.ds(other * TILE_ROWS, half * TILE_ROWS), :] = jnp.zeros(
                    (half * TILE_ROWS, LANES), y_ref.dtype)


def _moe_experts(items, xs, wg, bg, wu, bu, wd, bd):
    P = xs.shape[0] // TILE_ROWS
    tm = MOE_TILE
    n_items = items[0].shape[0]
    assert D_FF == D_MODEL
    tiles = pl.BlockSpec((tm * TILE_ROWS, LANES), lambda j, blk, e, lo, hi, par, nxt: (blk[j], 0))
    wspec = pl.BlockSpec(memory_space=pl.ANY)
    bspec = pl.BlockSpec((None, 1, D_FF), lambda j, blk, e, lo, hi, par, nxt: (e[j], 0, 0))
    grid_spec = pltpu.PrefetchScalarGridSpec(
        num_scalar_prefetch=6,
        grid=(n_items,),
        in_specs=[tiles, wspec, bspec, wspec, bspec, wspec, bspec],
        out_specs=tiles,
        scratch_shapes=[pltpu.VMEM((D_MODEL, D_FF), BF16),
                        pltpu.VMEM((D_MODEL, D_FF), BF16),
                        pltpu.VMEM((D_FF, D_MODEL), BF16),
                        pltpu.VMEM((2, 3, D_MODEL, D_FF), F32),
                        pltpu.SemaphoreType.DMA((2,))],
    )
    return pl.pallas_call(
        _moe_kernel,
        grid_spec=grid_spec,
        out_shape=jax.ShapeDtypeStruct((P * TILE_ROWS, LANES), F32),
        compiler_params=_params(("arbitrary",)),
        name="moe_experts",
    )(*items, xs, wg, bg, wu, bu, wd, bd)


def _combine_kernel(pos_ref, x1_ref, gt_ref, g_ref, ys_ref, o_ref, buf, sem):
    i = pl.program_id(0)
    n_tiles = pl.num_programs(0) - 1
    tm = x1_ref.shape[0]
    n_rows = TOP_K * tm

    @pl.when(i < n_tiles)
    def _():
        slot = i % 2
        for n in range(n_rows):
            src = ys_ref.at[pl.ds(pl.multiple_of(pos_ref[0, 0, n] * TILE_ROWS, TILE_ROWS), TILE_ROWS), :]
            pltpu.make_async_copy(src, buf.at[slot, pl.ds(n * TILE_ROWS, TILE_ROWS), :],
                                  sem.at[slot]).start(priority=n % 2)

    @pl.when(i > 0)
    def _():
        slot = (i - 1) % 2
        pltpu.make_async_copy(ys_ref.at[pl.ds(0, n_rows * TILE_ROWS), :], buf.at[slot], sem.at[slot]).wait()
        x = x1_ref[...]
        gates = gt_ref[...]
        for k in range(TOP_K):
            x = x + gates[:, k:k + 1] * _load_token_tiles(
                buf.at[slot, pl.ds(k * tm * TILE_ROWS, tm * TILE_ROWS), :], tm)
        ms = jnp.mean(x * x, axis=-1, keepdims=True)
        o_ref[...] = x * lax.rsqrt(ms + RMS_EPS) * g_ref[...]


def _combine(x1, ys, pos_tiles, gates, g):
    T = x1.shape[0]
    tm = COMBINE_TILE
    n_tiles = T // tm
    done = lambda i: (jnp.maximum(i - 1, 0), 0)
    return pl.pallas_call(
        _combine_kernel,
        grid=(n_tiles + 1,),
        in_specs=[pl.BlockSpec((1, 1, TOP_K * tm), lambda i: (jnp.minimum(i, n_tiles - 1), 0, 0),
                               memory_space=pltpu.SMEM),
                  pl.BlockSpec((tm, D_MODEL), done),
                  pl.BlockSpec((tm, TOP_K), done),
                  pl.BlockSpec((1, D_MODEL), lambda i: (0, 0)),
                  pl.BlockSpec(memory_space=pl.ANY)],
        out_specs=pl.BlockSpec((tm, D_MODEL), done),
        out_shape=jax.ShapeDtypeStruct((T, D_MODEL), F32),
        scratch_shapes=[pltpu.VMEM((2, TOP_K * tm * TILE_ROWS, LANES), F32),
                        pltpu.SemaphoreType.DMA((2,))],
        compiler_params=_params(("arbitrary",)),
        name="gather_combine_norm",
    )(pos_tiles, x1, gates, g, ys)


def kernel(x, norm1_g, w_in, lambda_q1, lambda_k1, lambda_q2, lambda_k2, subln_g, w_pool, b_pool,
           pool_scale, w_out, norm2_g, router_w, router_b, w_gate, b_gate, w_up, b_up, w_down,
           b_down, final_g):
    B, S, D = x.shape
    T = B * S
    l = 0
    lambda_init = 0.8 - 0.6 * math.exp(-0.3 * l)
    x2 = x.reshape(T, D)

    w = w_in[l]
    wqv_t = jnp.concatenate([w[:, :D_ATTN], w[:, 2 * D_ATTN:3 * D_ATTN]], axis=1).T.astype(BF16)
    qt, k, vt, u = _inproj(x2, norm1_g[l][None, :], wqv_t, w[:, D_ATTN:2 * D_ATTN].astype(BF16),
                           w[:, 3 * D_ATTN:].astype(BF16), B, S)
    attn = _attention(qt, k.reshape(B, S, D_ATTN), vt,
                      lambda_q1[l][None, :], lambda_k1[l][None, :],
                      lambda_q2[l][None, :], lambda_k2[l][None, :],
                      subln_g[l][None, :], lambda_init)
    pool = _pool(u.reshape(B, S, D_POOL), w_pool[l].astype(BF16),
                 b_pool[l].reshape(1, D_POOL), pool_scale[l][None, :])
    x1, h2, logits_t = _outproj(attn.reshape(T, D_ATTN), pool.reshape(T, D_POOL), x2,
                                w_out[l].astype(BF16), norm2_g[l][None, :],
                                router_w[l], router_b[l][None, :])

    pos_t, gates_t, counts = _route(logits_t)
    nt, _, W = pos_t.shape
    assert GATHER_TILE == COMBINE_TILE
    pos_tiles = (pos_t.reshape(nt, TOP_K, W // COMBINE_TILE, COMBINE_TILE).transpose(0, 2, 1, 3)
                 .reshape(T // COMBINE_TILE, 1, TOP_K * COMBINE_TILE))
    gates = gates_t.transpose(0, 2, 1).reshape(T, TOP_K)
    assert (T * TOP_K) % MOE_TILE == 0
    items = _item_table(counts[:, 0], T * TOP_K // MOE_TILE)

    xs = _scatter_rows(h2.reshape(T, TILE_ROWS, LANES), pos_tiles)
    ys = _moe_experts(items, xs.reshape(-1, LANES),
                      w_gate[l], b_gate[l][:, None, :], w_up[l], b_up[l][:, None, :],
                      w_down[l], b_down[l][:, None, :])
    out = _combine(x1, ys, pos_tiles, gates, final_g[None, :])
    return out.reshape(B, S, D)
```

```python
import functools
import math

import jax
import jax.numpy as jnp
from jax import lax
from jax.experimental import pallas as pl
from jax.experimental.pallas import tpu as pltpu

D_MODEL = 1024
D_ATTN = 512
D_POOL = 512
N_DIFF_HEADS = 4
DIFF_QK_DIM = 64
DIFF_V_DIM = 128
POOL_WINDOWS = (2, 4, 8, 16)
POOL_GROUP_DIM = 128
N_EXPERTS = 32
TOP_K = 4
D_FF = 1024
SWIGLU_LIMIT = 7.0
SWIGLU_ALPHA = 1.702
RMS_EPS = 1e-5

F32 = jnp.float32
BF16 = jnp.bfloat16

TOKEN_TILE = 512
ATTN_TILE = 256
MOE_TILE = 512
GATHER_TILE = 512
COMBINE_TILE = 256
FF_CHUNK = 512
VMEM_LIMIT = 56 * 1024 * 1024


def _params(sem, vmem=VMEM_LIMIT):
    return pltpu.CompilerParams(dimension_semantics=sem, vmem_limit_bytes=vmem)


LANES = 128
TILE_ROWS = D_MODEL // LANES
assert TILE_ROWS == 8


def _store_token_tiles(ref, x):
    n = x.shape[0]
    for c in range(TILE_ROWS):
        ref[pl.ds(c, n, stride=TILE_ROWS), :] = x[:, c * LANES:(c + 1) * LANES]


def _load_token_tiles(ref, n, dtype=None):
    cols = [ref[pl.ds(c, n, stride=TILE_ROWS), :] for c in range(TILE_ROWS)]
    if dtype is not None:
        cols = [c.astype(dtype) for c in cols]
    return jnp.concatenate(cols, axis=1)


def _inproj_kernel(x_ref, g_ref, wqv_ref, wk_ref, wu_ref, qt_ref, k_ref, vt_ref, u_ref):
    x = x_ref[...]
    ms = jnp.mean(x * x, axis=-1, keepdims=True)
    h = (x * lax.rsqrt(ms + RMS_EPS) * g_ref[...]).astype(BF16)
    qvt = lax.dot_general(wqv_ref[...], h, (((1,), (1,)), ((), ())), preferred_element_type=F32)
    qt_ref[...] = (qvt[:D_ATTN] * (DIFF_QK_DIM ** -0.5 * math.log2(math.e))).astype(BF16)
    vt_ref[...] = qvt[D_ATTN:].astype(BF16)
    k_ref[...] = jnp.dot(h, wk_ref[...], preferred_element_type=F32).astype(BF16)
    u_ref[...] = jnp.dot(h, wu_ref[...], preferred_element_type=F32)


def _inproj(x2, g, wqv_t, wk, wu, B, S):
    T = x2.shape[0]
    tm = TOKEN_TILE
    per_seq = S // tm
    row = lambda i: (i, 0)
    fixed = lambda i: (0, 0)
    tmap = lambda i: (i // per_seq, 0, i % per_seq)
    return pl.pallas_call(
        _inproj_kernel,
        grid=(T // tm,),
        in_specs=[pl.BlockSpec((tm, D_MODEL), row),
                  pl.BlockSpec((1, D_MODEL), fixed),
                  pl.BlockSpec((2 * D_ATTN, D_MODEL), fixed),
                  pl.BlockSpec((D_MODEL, D_ATTN), fixed),
                  pl.BlockSpec((D_MODEL, D_POOL), fixed)],
        out_specs=[pl.BlockSpec((None, D_ATTN, tm), tmap),
                   pl.BlockSpec((tm, D_ATTN), row),
                   pl.BlockSpec((None, D_ATTN, tm), tmap),
                   pl.BlockSpec((tm, D_POOL), row)],
        out_shape=[jax.ShapeDtypeStruct((B, D_ATTN, S), BF16),
                   jax.ShapeDtypeStruct((T, D_ATTN), BF16),
                   jax.ShapeDtypeStruct((B, D_ATTN, S), BF16),
                   jax.ShapeDtypeStruct((T, D_POOL), F32)],
        compiler_params=_params(("arbitrary",)),
        name="inproj",
    )(x2, g, wqv_t, wk, wu)


def _attn_kernel(qa_ref, qb_ref, k_ref, vt_ref, lq1_ref, lk1_ref, lq2_ref, lk2_ref, sg_ref,
                 o_ref, qs_s, vt_s, m_s, acc_s, s_s, *, lambda_init, n_q):
    tq = qa_ref.shape[1]
    tk = tq
    dv = DIFF_V_DIM
    p = pl.program_id(2)

    @pl.when(p == 0)
    def _():
        for c in range(n_q):
            vt_s[c, 0:dv, :] = vt_ref[:, c * tk:(c + 1) * tk]
            vt_s[c, dv:, :] = jnp.ones((vt_s.shape[1] - dv, tk), BF16)

    feat = lax.broadcasted_iota(jnp.int32, (dv, tq), 0)
    for blk, q_ref in enumerate((qa_ref, qb_ref)):
        qt = q_ref[...]
        zero = jnp.zeros_like(qt)
        qs_s[blk] = jnp.concatenate([jnp.where(feat < DIFF_QK_DIM, qt, zero),
                                     jnp.where(feat >= DIFF_QK_DIM, qt, zero)], axis=1)
    m_s[...] = jnp.full(m_s.shape, -jnp.inf, F32)
    acc_s[...] = jnp.zeros(acc_s.shape, F32)

    key = lax.broadcasted_iota(jnp.int32, (tk, tq), 0)
    qry = lax.broadcasted_iota(jnp.int32, (tk, tq), 1)
    bias = jnp.where(key <= qry, 0.0, -jnp.inf).astype(F32)
    bias = jnp.concatenate([bias, bias], axis=1)

    items = [(0, p, True), (1, n_q - 1 - p, True)]
    for n in range(n_q - 1):
        in_a = n < p
        items.append((jnp.where(in_a, 0, 1), jnp.where(in_a, n, n - p), False))

    def scores(blk, chunk, diag):
        kc = k_ref[pl.ds(pl.multiple_of(chunk * tk, tk), tk), :]
        s = jnp.dot(kc, qs_s[blk], preferred_element_type=F32)
        return s + bias if diag else s

    for t, (blk, chunk, diag) in enumerate(items):
        s = scores(blk, chunk, diag)
        s_s[t] = s
        m_s[blk] = jnp.maximum(m_s[blk], jnp.max(s, axis=0, keepdims=True))

    for t, (blk, chunk, diag) in enumerate(items):
        pt = jnp.exp2(s_s[t] - m_s[blk]).astype(BF16)
        acc_s[blk] += jnp.dot(vt_s[chunk], pt, preferred_element_type=F32)

    lam = (jnp.exp(jnp.sum(lq1_ref[...] * lk1_ref[...]))
           - jnp.exp(jnp.sum(lq2_ref[...] * lk2_ref[...])) + lambda_init)
    for blk, qblock in enumerate((p, n_q - 1 - p)):
        acc = acc_s[blk]
        o0 = acc[0:dv, 0:tq] / acc[dv:dv + 1, 0:tq]
        o1 = acc[0:dv, tq:] / acc[dv:dv + 1, tq:]
        a = o0 - lam * o1
        ms = jnp.mean(a * a, axis=0, keepdims=True)
        y = (a * lax.rsqrt(ms + RMS_EPS)).T * sg_ref[...]
        o_ref[pl.ds(pl.multiple_of(qblock * tq, tq), tq), :] = (y * (1.0 - lambda_init)).astype(o_ref.dtype)


def _attention(qt, k, vt, lq1, lk1, lq2, lk2, subln_g, lambda_init):
    B, _, S = qt.shape
    tq = ATTN_TILE
    n_q = S // tq
    assert S % tq == 0 and n_q % 2 == 0
    ones_rows = 16
    vec = lambda n: pl.BlockSpec((1, n), lambda b, h, p: (0, 0))
    return pl.pallas_call(
        functools.partial(_attn_kernel, lambda_init=lambda_init, n_q=n_q),
        grid=(B, N_DIFF_HEADS, n_q // 2),
        in_specs=[pl.BlockSpec((None, DIFF_V_DIM, tq), lambda b, h, p: (b, h, p)),
                  pl.BlockSpec((None, DIFF_V_DIM, tq), lambda b, h, p: (b, h, n_q - 1 - p)),
                  pl.BlockSpec((None, S, DIFF_V_DIM), lambda b, h, p: (b, 0, h)),
                  pl.BlockSpec((None, DIFF_V_DIM, S), lambda b, h, p: (b, h, 0)),
                  vec(DIFF_QK_DIM), vec(DIFF_QK_DIM), vec(DIFF_QK_DIM), vec(DIFF_QK_DIM),
                  vec(DIFF_V_DIM)],
        out_specs=pl.BlockSpec((None, S, DIFF_V_DIM), lambda b, h, p: (b, 0, h)),
        out_shape=jax.ShapeDtypeStruct((B, S, D_ATTN), BF16),
        scratch_shapes=[pltpu.VMEM((2, DIFF_V_DIM, 2 * tq), BF16),
                        pltpu.VMEM((n_q, DIFF_V_DIM + ones_rows, tq), BF16),
                        pltpu.VMEM((2, 1, 2 * tq), F32),
                        pltpu.VMEM((2, DIFF_V_DIM + ones_rows, 2 * tq), F32),
                        pltpu.VMEM((n_q + 1, tq, 2 * tq), F32)],
        compiler_params=_params(("arbitrary", "arbitrary", "arbitrary")),
        name="diff_attn",
    )(qt, qt, k, vt, lq1, lk1, lq2, lk2, subln_g)


def _pool_kernel(u_ref, w_ref, b_ref, sc_ref, o_ref):
    S = u_ref.shape[0]
    row = lax.broadcasted_iota(jnp.int32, (S, POOL_GROUP_DIM), 0)
    for g, w in enumerate(POOL_WINDOWS):
        cols = slice(g * POOL_GROUP_DIM, (g + 1) * POOL_GROUP_DIM)
        ug = u_ref[:, cols]
        s = ug
        span = 1
        while span < w:
            shifted = pltpu.roll(s, shift=span, axis=0)
            s = s + jnp.where(row >= span, shifted, 0.0)
            span *= 2
        cnt = jnp.minimum(row + 1, w).astype(F32)
        z = (s / cnt - ug).astype(BF16)
        y = jnp.dot(z, w_ref[g], preferred_element_type=F32) + b_ref[:, cols]
        o_ref[:, cols] = (y * sc_ref[:, cols]).astype(o_ref.dtype)


def _pool(u, w_pool, b_pool, pool_scale):
    B, S, _ = u.shape
    blk = pl.BlockSpec((None, S, D_POOL), lambda b: (b, 0, 0))
    return pl.pallas_call(
        _pool_kernel,
        grid=(B,),
        in_specs=[blk,
                  pl.BlockSpec((len(POOL_WINDOWS), POOL_GROUP_DIM, POOL_GROUP_DIM), lambda b: (0, 0, 0)),
                  pl.BlockSpec((1, D_POOL), lambda b: (0, 0)),
                  pl.BlockSpec((1, D_POOL), lambda b: (0, 0))],
        out_specs=blk,
        out_shape=jax.ShapeDtypeStruct((B, S, D_POOL), BF16),
        compiler_params=_params(("arbitrary",)),
        name="pool_mixer",
    )(u, w_pool, b_pool, pool_scale)


def _outproj_kernel(a_ref, p_ref, x_ref, w_ref, g_ref, rw_ref, rb_ref, x1_ref, h_ref, lg_ref):
    x1 = (x_ref[...]
          + jnp.dot(a_ref[...], w_ref[0:D_ATTN, :], preferred_element_type=F32)
          + jnp.dot(p_ref[...], w_ref[D_ATTN:, :], preferred_element_type=F32))
    x1_ref[...] = x1
    ms = jnp.mean(x1 * x1, axis=-1, keepdims=True)
    h = x1 * lax.rsqrt(ms + RMS_EPS) * g_ref[...]
    _store_token_tiles(h_ref, h)
    h_hi = h.astype(BF16)
    h_lo = (h - h_hi.astype(F32)).astype(BF16)
    both = jnp.dot(h_hi, rw_ref[...], preferred_element_type=F32)
    lg = (both[:, :LANES] + both[:, LANES:]
          + jnp.dot(h_lo, rw_ref[:, :LANES], preferred_element_type=F32) + rb_ref[...])
    lg_ref[...] = lg.T[:N_EXPERTS]


def _outproj(attn, pool, x2, w_out, g2, rw, rb):
    T = x2.shape[0]
    tm = TOKEN_TILE
    row = lambda i: (i, 0)
    fixed = lambda i: (0, 0)
    pad = ((0, 0), (0, LANES - N_EXPERTS))
    rw_hi = rw.astype(BF16)
    rw_lo = (rw - rw_hi.astype(F32)).astype(BF16)
    rw = jnp.concatenate([jnp.pad(rw_hi, pad), jnp.pad(rw_lo, pad)], axis=1)
    rb = jnp.pad(rb, pad)
    return pl.pallas_call(
        _outproj_kernel,
        grid=(T // tm,),
        in_specs=[pl.BlockSpec((tm, D_ATTN), row),
                  pl.BlockSpec((tm, D_POOL), row),
                  pl.BlockSpec((tm, D_MODEL), row),
                  pl.BlockSpec((D_MODEL, D_MODEL), fixed),
                  pl.BlockSpec((1, D_MODEL), fixed),
                  pl.BlockSpec((D_MODEL, 2 * LANES), fixed),
                  pl.BlockSpec((1, LANES), fixed)],
        out_specs=[pl.BlockSpec((tm, D_MODEL), row),
                   pl.BlockSpec((tm * TILE_ROWS, LANES), row),
                   pl.BlockSpec((None, N_EXPERTS, tm), lambda i: (i, 0, 0))],
        out_shape=[jax.ShapeDtypeStruct((T, D_MODEL), F32),
                   jax.ShapeDtypeStruct((T * TILE_ROWS, LANES), F32),
                   jax.ShapeDtypeStruct((T // tm, N_EXPERTS, tm), F32)],
        compiler_params=_params(("arbitrary",)),
        name="outproj_router",
    )(attn, pool, x2, w_out, g2, rw, rb)


def _route_kernel(lt_ref, pos_ref, gate_ref, cnt_ref, idx_s, rank_s):
    nt, E, W = lt_ref.shape
    e_iota = lax.broadcasted_iota(jnp.int32, (E, W), 0)
    before = (lax.broadcasted_iota(jnp.int32, (W, W), 0)
              < lax.broadcasted_iota(jnp.int32, (W, W), 1)).astype(BF16)
    ones = jnp.ones((W, LANES), BF16)
    widen = lambda a: jnp.concatenate([a] * (W // LANES), axis=1)

    def phase1(i, running):
        v = lt_ref[i]
        sel = jnp.zeros((E, W), F32)
        tops, hots = [], []
        for k in range(TOP_K):
            m = jnp.max(v, axis=0, keepdims=True)
            idx = jnp.min(jnp.where(v == m, e_iota, E), axis=0, keepdims=True)
            hot = e_iota == idx
            v = jnp.where(hot, -jnp.inf, v)
            sel = sel + hot.astype(F32)
            idx_s[i, k:k + 1, :] = idx
            tops.append(m)
            hots.append(hot)
        selb = sel.astype(BF16)
        rank = jnp.dot(selb, before, preferred_element_type=F32) + widen(running)
        for k in range(TOP_K):
            rank_s[i, k:k + 1, :] = jnp.sum(jnp.where(hots[k], rank, 0.0), axis=0, keepdims=True)
        ex = [jnp.exp(t - tops[0]) for t in tops]
        den = ex[0] + ex[1] + ex[2] + ex[3]
        for k in range(TOP_K):
            gate_ref[i, k:k + 1, :] = ex[k] / den
        return running + jnp.dot(selb, ones, preferred_element_type=F32)

    counts = lax.fori_loop(0, nt, phase1, jnp.zeros((E, LANES), F32))
    cnt_ref[...] = counts.astype(jnp.int32)

    row = lax.broadcasted_iota(jnp.int32, (E, LANES), 0)
    incl = counts
    span = 1
    while span < E:
        incl = incl + jnp.where(row >= span, pltpu.roll(incl, shift=span, axis=0), 0.0)
        span *= 2
    start = widen(incl - counts)

    def phase2(i, c):
        for k in range(TOP_K):
            hot = e_iota == idx_s[i, k:k + 1, :]
            base = jnp.sum(jnp.where(hot, start, 0.0), axis=0, keepdims=True)
            pos_ref[i, k:k + 1, :] = (base + rank_s[i, k:k + 1, :]).astype(jnp.int32)
        return c

    lax.fori_loop(0, nt, phase2, 0)


def _route(logits_t):
    nt, E, W = logits_t.shape
    whole = lambda shape: pl.BlockSpec(shape, lambda: (0,) * len(shape))
    return pl.pallas_call(
        _route_kernel,
        in_specs=[whole((nt, E, W))],
        out_specs=[whole((nt, TOP_K, W)), whole((nt, TOP_K, W)), whole((E, LANES))],
        out_shape=[jax.ShapeDtypeStruct((nt, TOP_K, W), jnp.int32),
                   jax.ShapeDtypeStruct((nt, TOP_K, W), F32),
                   jax.ShapeDtypeStruct((E, LANES), jnp.int32)],
        scratch_shapes=[pltpu.VMEM((nt, TOP_K, W), jnp.int32),
                        pltpu.VMEM((nt, TOP_K, W), F32)],
        compiler_params=pltpu.CompilerParams(vmem_limit_bytes=VMEM_LIMIT),
        name="route",
    )(logits_t)


def _work_items(counts):
    tm = MOE_TILE
    ends = jnp.cumsum(counts)
    starts = ends - counts
    first_blk = starts // tm
    last_blk = jnp.where(counts > 0, (ends - 1) // tm, first_blk - 1)
    n_items = last_blk - first_blk + 1
    item_end = jnp.cumsum(n_items)
    item_start = item_end - n_items
    return starts, ends, first_blk, item_start, item_end


def _item_table(counts, n_blocks):
    tm = MOE_TILE
    starts, ends, first_blk, item_start, item_end = _work_items(counts)
    n_slots = n_blocks + N_EXPERTS - 1
    j = jnp.arange(n_slots, dtype=jnp.int32)
    e = jnp.minimum(jnp.sum(item_end[None, :] <= j[:, None], axis=1), N_EXPERTS - 1).astype(jnp.int32)
    real = j < item_end[-1]
    blk = jnp.where(real, first_blk[e] + j - item_start[e], n_blocks - 1)
    lo = jnp.where(real, jnp.maximum(starts[e], blk * tm) - blk * tm, 0)
    hi = jnp.where(real, jnp.minimum(ends[e], (blk + 1) * tm) - blk * tm, 0)
    last_e = e[jnp.maximum(item_end[-1] - 1, 0)]
    e = jnp.where(real, e, last_e)
    ids = jnp.arange(N_EXPERTS, dtype=jnp.int32)
    used = counts > 0
    order = jnp.cumsum(used.astype(jnp.int32)) - 1
    later = jnp.where(jnp.logical_and(used[None, :], ids[None, :] > ids[:, None]), ids[None, :], N_EXPERTS)
    nxt_of = jnp.min(later, axis=1)
    nxt_of = jnp.where(nxt_of == N_EXPERTS, -1, nxt_of)
    as_i32 = lambda a: a.astype(jnp.int32)
    return as_i32(blk), as_i32(e), as_i32(lo), as_i32(hi), as_i32(order[e] % 2), as_i32(nxt_of[e])


def _scatter_kernel(pos_ref, src_ref, dst_ref, sem):
    tg = src_ref.shape[0]
    for r in range(tg):
        for k in range(TOP_K):
            pltpu.make_async_copy(src_ref.at[r], dst_ref.at[pos_ref[0, 0, k * tg + r]],
                                  sem).start(priority=(r * TOP_K + k) % 2)
    for k in range(TOP_K):
        pltpu.make_async_copy(src_ref, dst_ref.at[pl.ds(0, tg)], sem).wait()


def _scatter_rows(src, pos_tiles):
    T = src.shape[0]
    tg = GATHER_TILE
    return pl.pallas_call(
        _scatter_kernel,
        grid=(T // tg,),
        in_specs=[pl.BlockSpec((1, 1, tg * TOP_K), lambda i: (i, 0, 0), memory_space=pltpu.SMEM),
                  pl.BlockSpec((tg, TILE_ROWS, LANES), lambda i: (i, 0, 0))],
        out_specs=pl.BlockSpec(memory_space=pl.ANY),
        out_shape=jax.ShapeDtypeStruct((T * TOP_K, TILE_ROWS, LANES), src.dtype),
        scratch_shapes=[pltpu.SemaphoreType.DMA(())],
        compiler_params=_params(("arbitrary",)),
        name="row_scatter",
    )(pos_tiles, src)


def _moe_kernel(blk_ref, e_ref, lo_ref, hi_ref, par_ref, nxt_ref, xs_ref, wg_hbm, bg_ref, wu_hbm,
                bu_ref, wd_hbm, bd_ref, y_ref, wg_s, wu_s, wd_s, wbuf, wsem):
    j = pl.program_id(0)
    jp = jnp.maximum(j - 1, 0)
    tm = MOE_TILE
    srcs = (wg_hbm, wu_hbm, wd_hbm)

    def weight_copies(expert, buffer):
        return [pltpu.make_async_copy(src.at[expert], wbuf.at[buffer, m], wsem.at[buffer])
                for m, src in enumerate(srcs)]

    @pl.when(j == 0)
    def _():
        for cp in weight_copies(e_ref[0], par_ref[0]):
            cp.start()

    @pl.when(jnp.logical_or(j == 0, e_ref[j] != e_ref[jp]))
    def _():
        buffer = par_ref[j]
        for cp in weight_copies(e_ref[j], buffer):
            cp.wait()
        rows = 128
        for c in range(D_MODEL // rows):
            sl = slice(c * rows, (c + 1) * rows)
            wg_s[sl, :] = wbuf[buffer, 0, sl, :].astype(BF16)
            wu_s[sl, :] = wbuf[buffer, 1, sl, :].astype(BF16)
            wd_s[sl, :] = wbuf[buffer, 2, sl, :].astype(BF16)

        @pl.when(nxt_ref[j] >= 0)
        def _():
            for cp in weight_copies(nxt_ref[j], 1 - buffer):
                cp.start()

    lo = lo_ref[j]
    hi = hi_ref[j]

    first_visit = jnp.logical_or(j == 0, blk_ref[j] != blk_ref[jp])
    half = tm // 2

    def tile_rows(r0, n, c):
        return pl.ds(r0 * TILE_ROWS + c, n, stride=TILE_ROWS)

    def mlp(r0, n):
        x = jnp.concatenate([xs_ref[tile_rows(r0, n, c), :].astype(BF16) for c in range(TILE_ROWS)],
                            axis=1)
        y = None
        for f in range(D_FF // FF_CHUNK):
            cols = slice(f * FF_CHUNK, (f + 1) * FF_CHUNK)
            g = jnp.dot(x, wg_s[:, cols], preferred_element_type=F32) + bg_ref[:, cols]
            g = jnp.minimum(g, SWIGLU_LIMIT)
            u = jnp.dot(x, wu_s[:, cols], preferred_element_type=F32) + bu_ref[:, cols]
            u = jnp.clip(u, -SWIGLU_LIMIT, SWIGLU_LIMIT)
            act = (g * jax.nn.sigmoid(SWIGLU_ALPHA * g) * (u + 1.0)).astype(BF16)
            part = jnp.dot(act, wd_s[cols, :], preferred_element_type=F32)
            y = part if y is None else y + part
        return y + bd_ref[...]

    @pl.when(jnp.logical_and(lo == 0, hi == tm))
    def _():
        y = mlp(0, tm)
        for c in range(TILE_ROWS):
            y_ref[tile_rows(0, tm, c), :] = y[:, c * LANES:(c + 1) * LANES]

    def masked_rows(r0, n):
        y = mlp(r0, n)
        rows = r0 + lax.broadcasted_iota(jnp.int32, (n, LANES), 0)
        mine = jnp.logical_and(rows >= lo, rows < hi)

        @pl.when(first_visit)
        def _():
            for c in range(TILE_ROWS):
                y_ref[tile_rows(r0, n, c), :] = jnp.where(mine, y[:, c * LANES:(c + 1) * LANES], 0.0)

        @pl.when(jnp.logical_not(first_visit))
        def _():
            for c in range(TILE_ROWS):
                sl = tile_rows(r0, n, c)
                y_ref[sl, :] = jnp.where(mine, y[:, c * LANES:(c + 1) * LANES], y_ref[sl, :])

    partial = jnp.logical_and(lo < hi, jnp.logical_or(lo > 0, hi < tm))
    spans_both = jnp.logical_and(lo < half, hi > half)

    @pl.when(jnp.logical_and(partial, spans_both))
    def _():
        masked_rows(0, tm)

    for r0 in (0, half):
        inside = jnp.logical_and(lo >= r0, hi <= r0 + half)

        @pl.when(jnp.logical_and(partial, inside))
        def _():
            masked_rows(r0, half)

            @pl.when(first_visit)
            def _():
                other = half - r0
                y_ref[pl.ds(other * TILE_ROWS, half * TILE_ROWS), :] = jnp.zeros(
                    (half * TILE_ROWS, LANES), y_ref.dtype)


def _moe_experts(items, xs, wg, bg, wu, bu, wd, bd):
    P = xs.shape[0] // TILE_ROWS
    tm = MOE_TILE
    n_items = items[0].shape[0]
    assert D_FF == D_MODEL
    tiles = pl.BlockSpec((tm * TILE_ROWS, LANES), lambda j, blk, e, lo, hi, par, nxt: (blk[j], 0))
    wspec = pl.BlockSpec(memory_space=pl.ANY)
    bspec = pl.BlockSpec((None, 1, D_FF), lambda j, blk, e, lo, hi, par, nxt: (e[j], 0, 0))
    grid_spec = pltpu.PrefetchScalarGridSpec(
        num_scalar_prefetch=6,
        grid=(n_items,),
        in_specs=[tiles, wspec, bspec, wspec, bspec, wspec, bspec],
        out_specs=tiles,
        scratch_shapes=[pltpu.VMEM((D_MODEL, D_FF), BF16),
                        pltpu.VMEM((D_MODEL, D_FF), BF16),
                        pltpu.VMEM((D_FF, D_MODEL), BF16),
                        pltpu.VMEM((2, 3, D_MODEL, D_FF), F32),
                        pltpu.SemaphoreType.DMA((2,))],
    )
    return pl.pallas_call(
        _moe_kernel,
        grid_spec=grid_spec,
        out_shape=jax.ShapeDtypeStruct((P * TILE_ROWS, LANES), F32),
        compiler_params=_params(("arbitrary",)),
        name="moe_experts",
    )(*items, xs, wg, bg, wu, bu, wd, bd)


def _combine_kernel(pos_ref, x1_ref, gt_ref, g_ref, ys_ref, o_ref, buf, sem):
    i = pl.program_id(0)
    n_tiles = pl.num_programs(0) - 1
    tm = x1_ref.shape[0]
    n_rows = TOP_K * tm

    @pl.when(i < n_tiles)
    def _():
        slot = i % 2
        for n in range(n_rows):
            src = ys_ref.at[pl.ds(pl.multiple_of(pos_ref[0, 0, n] * TILE_ROWS, TILE_ROWS), TILE_ROWS), :]
            pltpu.make_async_copy(src, buf.at[slot, pl.ds(n * TILE_ROWS, TILE_ROWS), :],
                                  sem.at[slot]).start(priority=n % 2)

    @pl.when(i > 0)
    def _():
        slot = (i - 1) % 2
        pltpu.make_async_copy(ys_ref.at[pl.ds(0, n_rows * TILE_ROWS), :], buf.at[slot], sem.at[slot]).wait()
        x = x1_ref[...]
        gates = gt_ref[...]
        for k in range(TOP_K):
            x = x + gates[:, k:k + 1] * _load_token_tiles(
                buf.at[slot, pl.ds(k * tm * TILE_ROWS, tm * TILE_ROWS), :], tm)
        ms = jnp.mean(x * x, axis=-1, keepdims=True)
        o_ref[...] = x * lax.rsqrt(ms + RMS_EPS) * g_ref[...]


def _combine(x1, ys, pos_tiles, gates, g):
    T = x1.shape[0]
    tm = COMBINE_TILE
    n_tiles = T // tm
    done = lambda i: (jnp.maximum(i - 1, 0), 0)
    return pl.pallas_call(
        _combine_kernel,
        grid=(n_tiles + 1,),
        in_specs=[pl.BlockSpec((1, 1, TOP_K * tm), lambda i: (jnp.minimum(i, n_tiles - 1), 0, 0),
                               memory_space=pltpu.SMEM),
                  pl.BlockSpec((tm, D_MODEL), done),
                  pl.BlockSpec((tm, TOP_K), done),
                  pl.BlockSpec((1, D_MODEL), lambda i: (0, 0)),
                  pl.BlockSpec(memory_space=pl.ANY)],
        out_specs=pl.BlockSpec((tm, D_MODEL), done),
        out_shape=jax.ShapeDtypeStruct((T, D_MODEL), F32),
        scratch_shapes=[pltpu.VMEM((2, TOP_K * tm * TILE_ROWS, LANES), F32),
                        pltpu.SemaphoreType.DMA((2,))],
        compiler_params=_params(("arbitrary",)),
        name="gather_combine_norm",
    )(pos_tiles, x1, gates, g, ys)


def kernel(x, norm1_g, w_in, lambda_q1, lambda_k1, lambda_q2, lambda_k2, subln_g, w_pool, b_pool,
           pool_scale, w_out, norm2_g, router_w, router_b, w_gate, b_gate, w_up, b_up, w_down,
           b_down, final_g):
    B, S, D = x.shape
    T = B * S
    l = 0
    lambda_init = 0.8 - 0.6 * math.exp(-0.3 * l)
    x2 = x.reshape(T, D)

    w = w_in[l]
    wqv_t = jnp.concatenate([w[:, :D_ATTN], w[:, 2 * D_ATTN:3 * D_ATTN]], axis=1).T.astype(BF16)
    qt, k, vt, u = _inproj(x2, norm1_g[l][None, :], wqv_t, w[:, D_ATTN:2 * D_ATTN].astype(BF16),
                           w[:, 3 * D_ATTN:].astype(BF16), B, S)
    attn = _attention(qt, k.reshape(B, S, D_ATTN), vt,
                      lambda_q1[l][None, :], lambda_k1[l][None, :],
                      lambda_q2[l][None, :], lambda_k2[l][None, :],
                      subln_g[l][None, :], lambda_init)
    pool = _pool(u.reshape(B, S, D_POOL), w_pool[l].astype(BF16),
                 b_pool[l].reshape(1, D_POOL), pool_scale[l][None, :])
    x1, h2, logits_t = _outproj(attn.reshape(T, D_ATTN), pool.reshape(T, D_POOL), x2,
                                w_out[l].astype(BF16), norm2_g[l][None, :],
                                router_w[l], router_b[l][None, :])

    pos_t, gates_t, counts = _route(logits_t)
    nt, _, W = pos_t.shape

    def slot_tiles(tile):
        return (pos_t.reshape(nt, TOP_K, W // tile, tile).transpose(0, 2, 1, 3)
                .reshape(T // tile, 1, TOP_K * tile))

    gates = gates_t.transpose(0, 2, 1).reshape(T, TOP_K)
    assert (T * TOP_K) % MOE_TILE == 0
    items = _item_table(counts[:, 0], T * TOP_K // MOE_TILE)

    xs = _scatter_rows(h2.reshape(T, TILE_ROWS, LANES), slot_tiles(GATHER_TILE))
    ys = _moe_experts(items, xs.reshape(-1, LANES),
                      w_gate[l], b_gate[l][:, None, :], w_up[l], b_up[l][:, None, :],
                      w_down[l], b_down[l][:, None, :])
    out = _combine(x1, ys, slot_tiles(COMBINE_TILE), gates, final_g[None, :])
    return out.reshape(B, S, D)
```

```python
import functools
import math

import jax
import jax.numpy as jnp
from jax import lax
from jax.experimental import pallas as pl
from jax.experimental.pallas import tpu as pltpu

D_MODEL = 1024
D_ATTN = 512
D_POOL = 512
N_DIFF_HEADS = 4
DIFF_QK_DIM = 64
DIFF_V_DIM = 128
POOL_WINDOWS = (2, 4, 8, 16)
POOL_GROUP_DIM = 128
N_EXPERTS = 32
TOP_K = 4
D_FF = 1024
SWIGLU_LIMIT = 7.0
SWIGLU_ALPHA = 1.702
RMS_EPS = 1e-5

F32 = jnp.float32
BF16 = jnp.bfloat16

TOKEN_TILE = 512
ATTN_TILE = 256
MOE_TILE = 512
GATHER_TILE = 512
COMBINE_TILE = 256
FF_CHUNK = 512
VMEM_LIMIT = 56 * 1024 * 1024


def _params(sem, vmem=VMEM_LIMIT):
    return pltpu.CompilerParams(dimension_semantics=sem, vmem_limit_bytes=vmem)


LANES = 128
TILE_ROWS = D_MODEL // LANES
assert TILE_ROWS == 8


def _store_token_tiles(ref, x):
    n = x.shape[0]
    for c in range(TILE_ROWS):
        ref[pl.ds(c, n, stride=TILE_ROWS), :] = x[:, c * LANES:(c + 1) * LANES]


def _load_token_tiles(ref, n, dtype=None):
    cols = [ref[pl.ds(c, n, stride=TILE_ROWS), :] for c in range(TILE_ROWS)]
    if dtype is not None:
        cols = [c.astype(dtype) for c in cols]
    return jnp.concatenate(cols, axis=1)


def _inproj_kernel(x_ref, g_ref, wqv_ref, wk_ref, wu_ref, qt_ref, k_ref, vt_ref, u_ref):
    x = x_ref[...]
    ms = jnp.mean(x * x, axis=-1, keepdims=True)
    h = (x * lax.rsqrt(ms + RMS_EPS) * g_ref[...]).astype(BF16)
    qvt = lax.dot_general(wqv_ref[...], h, (((1,), (1,)), ((), ())), preferred_element_type=F32)
    qt_ref[...] = (qvt[:D_ATTN] * (DIFF_QK_DIM ** -0.5 * math.log2(math.e))).astype(BF16)
    vt_ref[...] = qvt[D_ATTN:].astype(BF16)
    k_ref[...] = jnp.dot(h, wk_ref[...], preferred_element_type=F32).astype(BF16)
    u_ref[...] = jnp.dot(h, wu_ref[...], preferred_element_type=F32)


def _inproj(x2, g, wqv_t, wk, wu, B, S):
    T = x2.shape[0]
    tm = TOKEN_TILE
    per_seq = S // tm
    row = lambda i: (i, 0)
    fixed = lambda i: (0, 0)
    tmap = lambda i: (i // per_seq, 0, i % per_seq)
    return pl.pallas_call(
        _inproj_kernel,
        grid=(T // tm,),
        in_specs=[pl.BlockSpec((tm, D_MODEL), row),
                  pl.BlockSpec((1, D_MODEL), fixed),
                  pl.BlockSpec((2 * D_ATTN, D_MODEL), fixed),
                  pl.BlockSpec((D_MODEL, D_ATTN), fixed),
                  pl.BlockSpec((D_MODEL, D_POOL), fixed)],
        out_specs=[pl.BlockSpec((None, D_ATTN, tm), tmap),
                   pl.BlockSpec((tm, D_ATTN), row),
                   pl.BlockSpec((None, D_ATTN, tm), tmap),
                   pl.BlockSpec((tm, D_POOL), row)],
        out_shape=[jax.ShapeDtypeStruct((B, D_ATTN, S), BF16),
                   jax.ShapeDtypeStruct((T, D_ATTN), BF16),
                   jax.ShapeDtypeStruct((B, D_ATTN, S), BF16),
                   jax.ShapeDtypeStruct((T, D_POOL), F32)],
        compiler_params=_params(("arbitrary",)),
        name="inproj",
    )(x2, g, wqv_t, wk, wu)


def _attn_kernel(qa_ref, qb_ref, k_ref, vt_ref, lq1_ref, lk1_ref, lq2_ref, lk2_ref, sg_ref,
                 o_ref, qs_s, vt_s, m_s, acc_s, s_s, *, lambda_init, n_q):
    tq = qa_ref.shape[1]
    tk = tq
    dv = DIFF_V_DIM
    p = pl.program_id(2)

    @pl.when(p == 0)
    def _():
        for c in range(n_q):
            vt_s[c, 0:dv, :] = vt_ref[:, c * tk:(c + 1) * tk]
            vt_s[c, dv:, :] = jnp.ones((vt_s.shape[1] - dv, tk), BF16)

    feat = lax.broadcasted_iota(jnp.int32, (dv, tq), 0)
    for blk, q_ref in enumerate((qa_ref, qb_ref)):
        qt = q_ref[...]
        zero = jnp.zeros_like(qt)
        qs_s[blk] = jnp.concatenate([jnp.where(feat < DIFF_QK_DIM, qt, zero),
                                     jnp.where(feat >= DIFF_QK_DIM, qt, zero)], axis=1)
    m_s[...] = jnp.full(m_s.shape, -jnp.inf, F32)
    acc_s[...] = jnp.zeros(acc_s.shape, F32)

    key = lax.broadcasted_iota(jnp.int32, (tk, tq), 0)
    qry = lax.broadcasted_iota(jnp.int32, (tk, tq), 1)
    bias = jnp.where(key <= qry, 0.0, -jnp.inf).astype(F32)
    bias = jnp.concatenate([bias, bias], axis=1)

    items = [(0, p, True), (1, n_q - 1 - p, True)]
    for n in range(n_q - 1):
        in_a = n < p
        items.append((jnp.where(in_a, 0, 1), jnp.where(in_a, n, n - p), False))

    def scores(blk, chunk, diag):
        kc = k_ref[pl.ds(pl.multiple_of(chunk * tk, tk), tk), :]
        s = jnp.dot(kc, qs_s[blk], preferred_element_type=F32)
        return s + bias if diag else s

    for t, (blk, chunk, diag) in enumerate(items):
        s = scores(blk, chunk, diag)
        s_s[t] = s
        m_s[blk] = jnp.maximum(m_s[blk], jnp.max(s, axis=0, keepdims=True))

    for t, (blk, chunk, diag) in enumerate(items):
        pt = jnp.exp2((s_s[t] - m_s[blk]).astype(BF16))
        acc_s[blk] += jnp.dot(vt_s[chunk], pt, preferred_element_type=F32)

    lam = (jnp.exp(jnp.sum(lq1_ref[...] * lk1_ref[...]))
           - jnp.exp(jnp.sum(lq2_ref[...] * lk2_ref[...])) + lambda_init)
    for blk, qblock in enumerate((p, n_q - 1 - p)):
        acc = acc_s[blk]
        o0 = acc[0:dv, 0:tq] / acc[dv:dv + 1, 0:tq]
        o1 = acc[0:dv, tq:] / acc[dv:dv + 1, tq:]
        a = o0 - lam * o1
        ms = jnp.mean(a * a, axis=0, keepdims=True)
        y = (a * lax.rsqrt(ms + RMS_EPS)).T * sg_ref[...]
        o_ref[pl.ds(pl.multiple_of(qblock * tq, tq), tq), :] = (y * (1.0 - lambda_init)).astype(o_ref.dtype)


def _attention(qt, k, vt, lq1, lk1, lq2, lk2, subln_g, lambda_init):
    B, _, S = qt.shape
    tq = ATTN_TILE
    n_q = S // tq
    assert S % tq == 0 and n_q % 2 == 0
    ones_rows = 16
    vec = lambda n: pl.BlockSpec((1, n), lambda b, h, p: (0, 0))
    return pl.pallas_call(
        functools.partial(_attn_kernel, lambda_init=lambda_init, n_q=n_q),
        grid=(B, N_DIFF_HEADS, n_q // 2),
        in_specs=[pl.BlockSpec((None, DIFF_V_DIM, tq), lambda b, h, p: (b, h, p)),
                  pl.BlockSpec((None, DIFF_V_DIM, tq), lambda b, h, p: (b, h, n_q - 1 - p)),
                  pl.BlockSpec((None, S, DIFF_V_DIM), lambda b, h, p: (b, 0, h)),
                  pl.BlockSpec((None, DIFF_V_DIM, S), lambda b, h, p: (b, h, 0)),
                  vec(DIFF_QK_DIM), vec(DIFF_QK_DIM), vec(DIFF_QK_DIM), vec(DIFF_QK_DIM),
                  vec(DIFF_V_DIM)],
        out_specs=pl.BlockSpec((None, S, DIFF_V_DIM), lambda b, h, p: (b, 0, h)),
        out_shape=jax.ShapeDtypeStruct((B, S, D_ATTN), BF16),
        scratch_shapes=[pltpu.VMEM((2, DIFF_V_DIM, 2 * tq), BF16),
                        pltpu.VMEM((n_q, DIFF_V_DIM + ones_rows, tq), BF16),
                        pltpu.VMEM((2, 1, 2 * tq), F32),
                        pltpu.VMEM((2, DIFF_V_DIM + ones_rows, 2 * tq), F32),
                        pltpu.VMEM((n_q + 1, tq, 2 * tq), F32)],
        compiler_params=_params(("arbitrary", "arbitrary", "arbitrary")),
        name="diff_attn",
    )(qt, qt, k, vt, lq1, lk1, lq2, lk2, subln_g)


def _pool_kernel(u_ref, w_ref, b_ref, sc_ref, o_ref):
    S = u_ref.shape[0]
    row = lax.broadcasted_iota(jnp.int32, (S, POOL_GROUP_DIM), 0)
    for g, w in enumerate(POOL_WINDOWS):
        cols = slice(g * POOL_GROUP_DIM, (g + 1) * POOL_GROUP_DIM)
        ug = u_ref[:, cols]
        s = ug
        span = 1
        while span < w:
            shifted = pltpu.roll(s, shift=span, axis=0)
            s = s + jnp.where(row >= span, shifted, 0.0)
            span *= 2
        cnt = jnp.minimum(row + 1, w).astype(F32)
        z = (s / cnt - ug).astype(BF16)
        y = jnp.dot(z, w_ref[g], preferred_element_type=F32) + b_ref[:, cols]
        o_ref[:, cols] = (y * sc_ref[:, cols]).astype(o_ref.dtype)


def _pool(u, w_pool, b_pool, pool_scale):
    B, S, _ = u.shape
    blk = pl.BlockSpec((None, S, D_POOL), lambda b: (b, 0, 0))
    return pl.pallas_call(
        _pool_kernel,
        grid=(B,),
        in_specs=[blk,
                  pl.BlockSpec((len(POOL_WINDOWS), POOL_GROUP_DIM, POOL_GROUP_DIM), lambda b: (0, 0, 0)),
                  pl.BlockSpec((1, D_POOL), lambda b: (0, 0)),
                  pl.BlockSpec((1, D_POOL), lambda b: (0, 0))],
        out_specs=blk,
        out_shape=jax.ShapeDtypeStruct((B, S, D_POOL), BF16),
        compiler_params=_params(("arbitrary",)),
        name="pool_mixer",
    )(u, w_pool, b_pool, pool_scale)


def _outproj_kernel(a_ref, p_ref, x_ref, w_ref, g_ref, rw_ref, rb_ref, x1_ref, h_ref, lg_ref):
    x1 = (x_ref[...]
          + jnp.dot(a_ref[...], w_ref[0:D_ATTN, :], preferred_element_type=F32)
          + jnp.dot(p_ref[...], w_ref[D_ATTN:, :], preferred_element_type=F32))
    x1_ref[...] = x1
    ms = jnp.mean(x1 * x1, axis=-1, keepdims=True)
    h = x1 * lax.rsqrt(ms + RMS_EPS) * g_ref[...]
    _store_token_tiles(h_ref, h)
    h_hi = h.astype(BF16)
    h_lo = (h - h_hi.astype(F32)).astype(BF16)
    both = jnp.dot(h_hi, rw_ref[...], preferred_element_type=F32)
    lg = (both[:, :LANES] + both[:, LANES:]
          + jnp.dot(h_lo, rw_ref[:, :LANES], preferred_element_type=F32) + rb_ref[...])
    lg_ref[...] = lg.T[:N_EXPERTS]


def _outproj(attn, pool, x2, w_out, g2, rw, rb):
    T = x2.shape[0]
    tm = TOKEN_TILE
    row = lambda i: (i, 0)
    fixed = lambda i: (0, 0)
    pad = ((0, 0), (0, LANES - N_EXPERTS))
    rw_hi = rw.astype(BF16)
    rw_lo = (rw - rw_hi.astype(F32)).astype(BF16)
    rw = jnp.concatenate([jnp.pad(rw_hi, pad), jnp.pad(rw_lo, pad)], axis=1)
    rb = jnp.pad(rb, pad)
    return pl.pallas_call(
        _outproj_kernel,
        grid=(T // tm,),
        in_specs=[pl.BlockSpec((tm, D_ATTN), row),
                  pl.BlockSpec((tm, D_POOL), row),
                  pl.BlockSpec((tm, D_MODEL), row),
                  pl.BlockSpec((D_MODEL, D_MODEL), fixed),
                  pl.BlockSpec((1, D_MODEL), fixed),
                  pl.BlockSpec((D_MODEL, 2 * LANES), fixed),
                  pl.BlockSpec((1, LANES), fixed)],
        out_specs=[pl.BlockSpec((tm, D_MODEL), row),
                   pl.BlockSpec((tm * TILE_ROWS, LANES), row),
                   pl.BlockSpec((None, N_EXPERTS, tm), lambda i: (i, 0, 0))],
        out_shape=[jax.ShapeDtypeStruct((T, D_MODEL), F32),
                   jax.ShapeDtypeStruct((T * TILE_ROWS, LANES), F32),
                   jax.ShapeDtypeStruct((T // tm, N_EXPERTS, tm), F32)],
        compiler_params=_params(("arbitrary",)),
        name="outproj_router",
    )(attn, pool, x2, w_out, g2, rw, rb)


def _route_kernel(lt_ref, pos_ref, gate_ref, cnt_ref, idx_s, rank_s):
    nt, E, W = lt_ref.shape
    e_iota = lax.broadcasted_iota(jnp.int32, (E, W), 0)
    before = (lax.broadcasted_iota(jnp.int32, (W, W), 0)
              < lax.broadcasted_iota(jnp.int32, (W, W), 1)).astype(BF16)
    ones = jnp.ones((W, LANES), BF16)
    widen = lambda a: jnp.concatenate([a] * (W // LANES), axis=1)

    def phase1(i, running):
        v = lt_ref[i]
        sel = jnp.zeros((E, W), F32)
        tops, hots = [], []
        for k in range(TOP_K):
            m = jnp.max(v, axis=0, keepdims=True)
            idx = jnp.min(jnp.where(v == m, e_iota, E), axis=0, keepdims=True)
            hot = e_iota == idx
            v = jnp.where(hot, -jnp.inf, v)
            sel = sel + hot.astype(F32)
            idx_s[i, k:k + 1, :] = idx
            tops.append(m)
            hots.append(hot)
        selb = sel.astype(BF16)
        rank = jnp.dot(selb, before, preferred_element_type=F32) + widen(running)
        for k in range(TOP_K):
            rank_s[i, k:k + 1, :] = jnp.sum(jnp.where(hots[k], rank, 0.0), axis=0, keepdims=True)
        ex = [jnp.exp(t - tops[0]) for t in tops]
        den = ex[0] + ex[1] + ex[2] + ex[3]
        for k in range(TOP_K):
            gate_ref[i, k:k + 1, :] = ex[k] / den
        return running + jnp.dot(selb, ones, preferred_element_type=F32)

    counts = lax.fori_loop(0, nt, phase1, jnp.zeros((E, LANES), F32))
    cnt_ref[...] = counts.astype(jnp.int32)

    row = lax.broadcasted_iota(jnp.int32, (E, LANES), 0)
    incl = counts
    span = 1
    while span < E:
        incl = incl + jnp.where(row >= span, pltpu.roll(incl, shift=span, axis=0), 0.0)
        span *= 2
    start = widen(incl - counts)

    def phase2(i, c):
        for k in range(TOP_K):
            hot = e_iota == idx_s[i, k:k + 1, :]
            base = jnp.sum(jnp.where(hot, start, 0.0), axis=0, keepdims=True)
            pos_ref[i, k:k + 1, :] = (base + rank_s[i, k:k + 1, :]).astype(jnp.int32)
        return c

    lax.fori_loop(0, nt, phase2, 0)


def _route(logits_t):
    nt, E, W = logits_t.shape
    whole = lambda shape: pl.BlockSpec(shape, lambda: (0,) * len(shape))
    return pl.pallas_call(
        _route_kernel,
        in_specs=[whole((nt, E, W))],
        out_specs=[whole((nt, TOP_K, W)), whole((nt, TOP_K, W)), whole((E, LANES))],
        out_shape=[jax.ShapeDtypeStruct((nt, TOP_K, W), jnp.int32),
                   jax.ShapeDtypeStruct((nt, TOP_K, W), F32),
                   jax.ShapeDtypeStruct((E, LANES), jnp.int32)],
        scratch_shapes=[pltpu.VMEM((nt, TOP_K, W), jnp.int32),
                        pltpu.VMEM((nt, TOP_K, W), F32)],
        compiler_params=pltpu.CompilerParams(vmem_limit_bytes=VMEM_LIMIT),
        name="route",
    )(logits_t)


def _work_items(counts):
    tm = MOE_TILE
    ends = jnp.cumsum(counts)
    starts = ends - counts
    first_blk = starts // tm
    last_blk = jnp.where(counts > 0, (ends - 1) // tm, first_blk - 1)
    n_items = last_blk - first_blk + 1
    item_end = jnp.cumsum(n_items)
    item_start = item_end - n_items
    return starts, ends, first_blk, item_start, item_end


def _item_table(counts, n_blocks):
    tm = MOE_TILE
    starts, ends, first_blk, item_start, item_end = _work_items(counts)
    n_slots = n_blocks + N_EXPERTS - 1
    j = jnp.arange(n_slots, dtype=jnp.int32)
    e = jnp.minimum(jnp.sum(item_end[None, :] <= j[:, None], axis=1), N_EXPERTS - 1).astype(jnp.int32)
    real = j < item_end[-1]
    blk = jnp.where(real, first_blk[e] + j - item_start[e], n_blocks - 1)
    lo = jnp.where(real, jnp.maximum(starts[e], blk * tm) - blk * tm, 0)
    hi = jnp.where(real, jnp.minimum(ends[e], (blk + 1) * tm) - blk * tm, 0)
    last_e = e[jnp.maximum(item_end[-1] - 1, 0)]
    e = jnp.where(real, e, last_e)
    ids = jnp.arange(N_EXPERTS, dtype=jnp.int32)
    used = counts > 0
    order = jnp.cumsum(used.astype(jnp.int32)) - 1
    later = jnp.where(jnp.logical_and(used[None, :], ids[None, :] > ids[:, None]), ids[None, :], N_EXPERTS)
    nxt_of = jnp.min(later, axis=1)
    nxt_of = jnp.where(nxt_of == N_EXPERTS, -1, nxt_of)
    as_i32 = lambda a: a.astype(jnp.int32)
    return as_i32(blk), as_i32(e), as_i32(lo), as_i32(hi), as_i32(order[e] % 2), as_i32(nxt_of[e])


def _scatter_kernel(pos_ref, src_ref, dst_ref, sem):
    tg = src_ref.shape[0]
    for r in range(tg):
        for k in range(TOP_K):
            pltpu.make_async_copy(src_ref.at[r], dst_ref.at[pos_ref[0, 0, k * tg + r]],
                                  sem).start(priority=(r * TOP_K + k) % 2)
    for k in range(TOP_K):
        pltpu.make_async_copy(src_ref, dst_ref.at[pl.ds(0, tg)], sem).wait()


def _scatter_rows(src, pos_tiles):
    T = src.shape[0]
    tg = GATHER_TILE
    return pl.pallas_call(
        _scatter_kernel,
        grid=(T // tg,),
        in_specs=[pl.BlockSpec((1, 1, tg * TOP_K), lambda i: (i, 0, 0), memory_space=pltpu.SMEM),
                  pl.BlockSpec((tg, TILE_ROWS, LANES), lambda i: (i, 0, 0))],
        out_specs=pl.BlockSpec(memory_space=pl.ANY),
        out_shape=jax.ShapeDtypeStruct((T * TOP_K, TILE_ROWS, LANES), src.dtype),
        scratch_shapes=[pltpu.SemaphoreType.DMA(())],
        compiler_params=_params(("arbitrary",)),
        name="row_scatter",
    )(pos_tiles, src)


def _moe_kernel(blk_ref, e_ref, lo_ref, hi_ref, par_ref, nxt_ref, xs_ref, wg_hbm, bg_ref, wu_hbm,
                bu_ref, wd_hbm, bd_ref, y_ref, wg_s, wu_s, wd_s, wbuf, wsem):
    j = pl.program_id(0)
    jp = jnp.maximum(j - 1, 0)
    tm = MOE_TILE
    srcs = (wg_hbm, wu_hbm, wd_hbm)

    def weight_copies(expert, buffer):
        return [pltpu.make_async_copy(src.at[expert], wbuf.at[buffer, m], wsem.at[buffer])
                for m, src in enumerate(srcs)]

    @pl.when(j == 0)
    def _():
        for cp in weight_copies(e_ref[0], par_ref[0]):
            cp.start()

    @pl.when(jnp.logical_or(j == 0, e_ref[j] != e_ref[jp]))
    def _():
        buffer = par_ref[j]
        for cp in weight_copies(e_ref[j], buffer):
            cp.wait()
        rows = 128
        for c in range(D_MODEL // rows):
            sl = slice(c * rows, (c + 1) * rows)
            wg_s[sl, :] = wbuf[buffer, 0, sl, :].astype(BF16)
            wu_s[sl, :] = wbuf[buffer, 1, sl, :].astype(BF16)
            wd_s[sl, :] = wbuf[buffer, 2, sl, :].astype(BF16)

        @pl.when(nxt_ref[j] >= 0)
        def _():
            for cp in weight_copies(nxt_ref[j], 1 - buffer):
                cp.start()

    lo = lo_ref[j]
    hi = hi_ref[j]

    first_visit = jnp.logical_or(j == 0, blk_ref[j] != blk_ref[jp])
    half = tm // 2

    def tile_rows(r0, n, c):
        return pl.ds(r0 * TILE_ROWS + c, n, stride=TILE_ROWS)

    def mlp(r0, n):
        x = jnp.concatenate([xs_ref[tile_rows(r0, n, c), :].astype(BF16) for c in range(TILE_ROWS)],
                            axis=1)
        y = None
        for f in range(D_FF // FF_CHUNK):
            cols = slice(f * FF_CHUNK, (f + 1) * FF_CHUNK)
            g = jnp.dot(x, wg_s[:, cols], preferred_element_type=F32) + bg_ref[:, cols]
            g = jnp.minimum(g, SWIGLU_LIMIT)
            u = jnp.dot(x, wu_s[:, cols], preferred_element_type=F32) + bu_ref[:, cols]
            u = jnp.clip(u, -SWIGLU_LIMIT, SWIGLU_LIMIT)
            act = (g * jax.nn.sigmoid(SWIGLU_ALPHA * g) * (u + 1.0)).astype(BF16)
            part = jnp.dot(act, wd_s[cols, :], preferred_element_type=F32)
            y = part if y is None else y + part
        return y + bd_ref[...]

    @pl.when(jnp.logical_and(lo == 0, hi == tm))
    def _():
        y = mlp(0, tm)
        for c in range(TILE_ROWS):
            y_ref[tile_rows(0, tm, c), :] = y[:, c * LANES:(c + 1) * LANES]

    def masked_rows(r0, n):
        y = mlp(r0, n)
        rows = r0 + lax.broadcasted_iota(jnp.int32, (n, LANES), 0)
        mine = jnp.logical_and(rows >= lo, rows < hi)

        @pl.when(first_visit)
        def _():
            for c in range(TILE_ROWS):
                y_ref[tile_rows(r0, n, c), :] = jnp.where(mine, y[:, c * LANES:(c + 1) * LANES], 0.0)

        @pl.when(jnp.logical_not(first_visit))
        def _():
            for c in range(TILE_ROWS):
                sl = tile_rows(r0, n, c)
                y_ref[sl, :] = jnp.where(mine, y[:, c * LANES:(c + 1) * LANES], y_ref[sl, :])

    partial = jnp.logical_and(lo < hi, jnp.logical_or(lo > 0, hi < tm))
    spans_both = jnp.logical_and(lo < half, hi > half)

    @pl.when(jnp.logical_and(partial, spans_both))
    def _():
        masked_rows(0, tm)

    for r0 in (0, half):
        inside = jnp.logical_and(lo >= r0, hi <= r0 + half)

        @pl.when(jnp.logical_and(partial, inside))
        def _():
            masked_rows(r0, half)

            @pl.when(first_visit)
            def _():
                other = half - r0
                y_ref[pl.ds(other * TILE_ROWS, half * TILE_ROWS), :] = jnp.zeros(
                    (half * TILE_ROWS, LANES), y_ref.dtype)


def _moe_experts(items, xs, wg, bg, wu, bu, wd, bd):
    P = xs.shape[0] // TILE_ROWS
    tm = MOE_TILE
    n_items = items[0].shape[0]
    assert D_FF == D_MODEL
    tiles = pl.BlockSpec((tm * TILE_ROWS, LANES), lambda j, blk, e, lo, hi, par, nxt: (blk[j], 0))
    wspec = pl.BlockSpec(memory_space=pl.ANY)
    bspec = pl.BlockSpec((None, 1, D_FF), lambda j, blk, e, lo, hi, par, nxt: (e[j], 0, 0))
    grid_spec = pltpu.PrefetchScalarGridSpec(
        num_scalar_prefetch=6,
        grid=(n_items,),
        in_specs=[tiles, wspec, bspec, wspec, bspec, wspec, bspec],
        out_specs=tiles,
        scratch_shapes=[pltpu.VMEM((D_MODEL, D_FF), BF16),
                        pltpu.VMEM((D_MODEL, D_FF), BF16),
                        pltpu.VMEM((D_FF, D_MODEL), BF16),
                        pltpu.VMEM((2, 3, D_MODEL, D_FF), F32),
                        pltpu.SemaphoreType.DMA((2,))],
    )
    return pl.pallas_call(
        _moe_kernel,
        grid_spec=grid_spec,
        out_shape=jax.ShapeDtypeStruct((P * TILE_ROWS, LANES), F32),
        compiler_params=_params(("arbitrary",)),
        name="moe_experts",
    )(*items, xs, wg, bg, wu, bu, wd, bd)


def _combine_kernel(pos_ref, x1_ref, gt_ref, g_ref, ys_ref, o_ref, buf, sem):
    i = pl.program_id(0)
    n_tiles = pl.num_programs(0) - 1
    tm = x1_ref.shape[0]
    n_rows = TOP_K * tm

    @pl.when(i < n_tiles)
    def _():
        slot = i % 2
        for n in range(n_rows):
            src = ys_ref.at[pl.ds(pl.multiple_of(pos_ref[0, 0, n] * TILE_ROWS, TILE_ROWS), TILE_ROWS), :]
            pltpu.make_async_copy(src, buf.at[slot, pl.ds(n * TILE_ROWS, TILE_ROWS), :],
                                  sem.at[slot]).start(priority=n % 2)

    @pl.when(i > 0)
    def _():
        slot = (i - 1) % 2
        pltpu.make_async_copy(ys_ref.at[pl.ds(0, n_rows * TILE_ROWS), :], buf.at[slot], sem.at[slot]).wait()
        x = x1_ref[...]
        gates = gt_ref[...]
        for k in range(TOP_K):
            x = x + gates[:, k:k + 1] * _load_token_tiles(
                buf.at[slot, pl.ds(k * tm * TILE_ROWS, tm * TILE_ROWS), :], tm)
        ms = jnp.mean(x * x, axis=-1, keepdims=True)
        o_ref[...] = x * lax.rsqrt(ms + RMS_EPS) * g_ref[...]


def _combine(x1, ys, pos_tiles, gates, g):
    T = x1.shape[0]
    tm = COMBINE_TILE
    n_tiles = T // tm
    done = lambda i: (jnp.maximum(i - 1, 0), 0)
    return pl.pallas_call(
        _combine_kernel,
        grid=(n_tiles + 1,),
        in_specs=[pl.BlockSpec((1, 1, TOP_K * tm), lambda i: (jnp.minimum(i, n_tiles - 1), 0, 0),
                               memory_space=pltpu.SMEM),
                  pl.BlockSpec((tm, D_MODEL), done),
                  pl.BlockSpec((tm, TOP_K), done),
                  pl.BlockSpec((1, D_MODEL), lambda i: (0, 0)),
                  pl.BlockSpec(memory_space=pl.ANY)],
        out_specs=pl.BlockSpec((tm, D_MODEL), done),
        out_shape=jax.ShapeDtypeStruct((T, D_MODEL), F32),
        scratch_shapes=[pltpu.VMEM((2, TOP_K * tm * TILE_ROWS, LANES), F32),
                        pltpu.SemaphoreType.DMA((2,))],
        compiler_params=_params(("arbitrary",)),
        name="gather_combine_norm",
    )(pos_tiles, x1, gates, g, ys)


def kernel(x, norm1_g, w_in, lambda_q1, lambda_k1, lambda_q2, lambda_k2, subln_g, w_pool, b_pool,
           pool_scale, w_out, norm2_g, router_w, router_b, w_gate, b_gate, w_up, b_up, w_down,
           b_down, final_g):
    B, S, D = x.shape
    T = B * S
    l = 0
    lambda_init = 0.8 - 0.6 * math.exp(-0.3 * l)
    x2 = x.reshape(T, D)

    w = w_in[l]
    wqv_t = jnp.concatenate([w[:, :D_ATTN], w[:, 2 * D_ATTN:3 * D_ATTN]], axis=1).T.astype(BF16)
    qt, k, vt, u = _inproj(x2, norm1_g[l][None, :], wqv_t, w[:, D_ATTN:2 * D_ATTN].astype(BF16),
                           w[:, 3 * D_ATTN:].astype(BF16), B, S)
    attn = _attention(qt, k.reshape(B, S, D_ATTN), vt,
                      lambda_q1[l][None, :], lambda_k1[l][None, :],
                      lambda_q2[l][None, :], lambda_k2[l][None, :],
                      subln_g[l][None, :], lambda_init)
    pool = _pool(u.reshape(B, S, D_POOL), w_pool[l].astype(BF16),
                 b_pool[l].reshape(1, D_POOL), pool_scale[l][None, :])
    x1, h2, logits_t = _outproj(attn.reshape(T, D_ATTN), pool.reshape(T, D_POOL), x2,
                                w_out[l].astype(BF16), norm2_g[l][None, :],
                                router_w[l], router_b[l][None, :])

    pos_t, gates_t, counts = _route(logits_t)
    nt, _, W = pos_t.shape

    def slot_tiles(tile):
        return (pos_t.reshape(nt, TOP_K, W // tile, tile).transpose(0, 2, 1, 3)
                .reshape(T // tile, 1, TOP_K * tile))

    gates = gates_t.transpose(0, 2, 1).reshape(T, TOP_K)
    assert (T * TOP_K) % MOE_TILE == 0
    items = _item_table(counts[:, 0], T * TOP_K // MOE_TILE)

    xs = _scatter_rows(h2.reshape(T, TILE_ROWS, LANES), slot_tiles(GATHER_TILE))
    ys = _moe_experts(items, xs.reshape(-1, LANES),
                      w_gate[l], b_gate[l][:, None, :], w_up[l], b_up[l][:, None, :],
                      w_down[l], b_down[l][:, None, :])
    out = _combine(x1, ys, slot_tiles(COMBINE_TILE), gates, final_g[None, :])
    return out.reshape(B, S, D)
```

```python
import functools
import math

import jax
import jax.numpy as jnp
from jax import lax
from jax.experimental import pallas as pl
from jax.experimental.pallas import tpu as pltpu

D_MODEL = 1024
D_ATTN = 512
D_POOL = 512
N_DIFF_HEADS = 4
DIFF_QK_DIM = 64
DIFF_V_DIM = 128
POOL_WINDOWS = (2, 4, 8, 16)
POOL_GROUP_DIM = 128
N_EXPERTS = 32
TOP_K = 4
D_FF = 1024
SWIGLU_LIMIT = 7.0
SWIGLU_ALPHA = 1.702
RMS_EPS = 1e-5

F32 = jnp.float32
BF16 = jnp.bfloat16

TOKEN_TILE = 512
ATTN_TILE = 512
MOE_TILE = 512
GATHER_TILE = 512
COMBINE_TILE = 256
FF_CHUNK = 512
VMEM_LIMIT = 56 * 1024 * 1024


def _params(sem, vmem=VMEM_LIMIT):
    return pltpu.CompilerParams(dimension_semantics=sem, vmem_limit_bytes=vmem)


LANES = 128
TILE_ROWS = D_MODEL // LANES
assert TILE_ROWS == 8


def _store_token_tiles(ref, x):
    n = x.shape[0]
    for c in range(TILE_ROWS):
        ref[pl.ds(c, n, stride=TILE_ROWS), :] = x[:, c * LANES:(c + 1) * LANES]


def _load_token_tiles(ref, n, dtype=None):
    cols = [ref[pl.ds(c, n, stride=TILE_ROWS), :] for c in range(TILE_ROWS)]
    if dtype is not None:
        cols = [c.astype(dtype) for c in cols]
    return jnp.concatenate(cols, axis=1)


def _inproj_kernel(x_ref, g_ref, wqv_ref, wk_ref, wu_ref, qt_ref, k_ref, vt_ref, u_ref):
    x = x_ref[...]
    ms = jnp.mean(x * x, axis=-1, keepdims=True)
    h = (x * lax.rsqrt(ms + RMS_EPS) * g_ref[...]).astype(BF16)
    qvt = lax.dot_general(wqv_ref[...], h, (((1,), (1,)), ((), ())), preferred_element_type=F32)
    qt_ref[...] = (qvt[:D_ATTN] * (DIFF_QK_DIM ** -0.5 * math.log2(math.e))).astype(BF16)
    vt_ref[...] = qvt[D_ATTN:].astype(BF16)
    k_ref[...] = jnp.dot(h, wk_ref[...], preferred_element_type=F32).astype(BF16)
    u_ref[...] = jnp.dot(h, wu_ref[...], preferred_element_type=F32)


def _inproj(x2, g, wqv_t, wk, wu, B, S):
    T = x2.shape[0]
    tm = TOKEN_TILE
    per_seq = S // tm
    row = lambda i: (i, 0)
    fixed = lambda i: (0, 0)
    tmap = lambda i: (i // per_seq, 0, i % per_seq)
    return pl.pallas_call(
        _inproj_kernel,
        grid=(T // tm,),
        in_specs=[pl.BlockSpec((tm, D_MODEL), row),
                  pl.BlockSpec((1, D_MODEL), fixed),
                  pl.BlockSpec((2 * D_ATTN, D_MODEL), fixed),
                  pl.BlockSpec((D_MODEL, D_ATTN), fixed),
                  pl.BlockSpec((D_MODEL, D_POOL), fixed)],
        out_specs=[pl.BlockSpec((None, D_ATTN, tm), tmap),
                   pl.BlockSpec((tm, D_ATTN), row),
                   pl.BlockSpec((None, D_ATTN, tm), tmap),
                   pl.BlockSpec((tm, D_POOL), row)],
        out_shape=[jax.ShapeDtypeStruct((B, D_ATTN, S), BF16),
                   jax.ShapeDtypeStruct((T, D_ATTN), BF16),
                   jax.ShapeDtypeStruct((B, D_ATTN, S), BF16),
                   jax.ShapeDtypeStruct((T, D_POOL), F32)],
        compiler_params=_params(("arbitrary",)),
        name="inproj",
    )(x2, g, wqv_t, wk, wu)


def _attn_kernel(qa_ref, qb_ref, k_ref, vt_ref, lq1_ref, lk1_ref, lq2_ref, lk2_ref, sg_ref,
                 o_ref, qs_s, vt_s, m_s, acc_s, s_s, *, lambda_init, n_q):
    tq = qa_ref.shape[1]
    tk = tq
    dv = DIFF_V_DIM
    p = pl.program_id(2)

    @pl.when(p == 0)
    def _():
        for c in range(n_q):
            vt_s[c, 0:dv, :] = vt_ref[:, c * tk:(c + 1) * tk]
            vt_s[c, dv:, :] = jnp.ones((vt_s.shape[1] - dv, tk), BF16)

    feat = lax.broadcasted_iota(jnp.int32, (dv, tq), 0)
    for blk, q_ref in enumerate((qa_ref, qb_ref)):
        qt = q_ref[...]
        zero = jnp.zeros_like(qt)
        qs_s[blk] = jnp.concatenate([jnp.where(feat < DIFF_QK_DIM, qt, zero),
                                     jnp.where(feat >= DIFF_QK_DIM, qt, zero)], axis=1)
    m_s[...] = jnp.full(m_s.shape, -jnp.inf, F32)
    acc_s[...] = jnp.zeros(acc_s.shape, F32)

    key = lax.broadcasted_iota(jnp.int32, (tk, tq), 0)
    qry = lax.broadcasted_iota(jnp.int32, (tk, tq), 1)
    bias = jnp.where(key <= qry, 0.0, -jnp.inf).astype(F32)
    bias = jnp.concatenate([bias, bias], axis=1)

    items = [(0, p, True), (1, n_q - 1 - p, True)]
    for n in range(n_q - 1):
        in_a = n < p
        items.append((jnp.where(in_a, 0, 1), jnp.where(in_a, n, n - p), False))

    def scores(blk, chunk, diag):
        kc = k_ref[pl.ds(pl.multiple_of(chunk * tk, tk), tk), :]
        s = jnp.dot(kc, qs_s[blk], preferred_element_type=F32)
        return s + bias if diag else s

    for t, (blk, chunk, diag) in enumerate(items):
        s = scores(blk, chunk, diag)
        s_s[t] = s
        m_s[blk] = jnp.maximum(m_s[blk], jnp.max(s, axis=0, keepdims=True))

    for t, (blk, chunk, diag) in enumerate(items):
        pt = jnp.exp2(s_s[t] - m_s[blk]).astype(BF16)
        acc_s[blk] += jnp.dot(vt_s[chunk], pt, preferred_element_type=F32)

    lam = (jnp.exp(jnp.sum(lq1_ref[...] * lk1_ref[...]))
           - jnp.exp(jnp.sum(lq2_ref[...] * lk2_ref[...])) + lambda_init)
    for blk, qblock in enumerate((p, n_q - 1 - p)):
        acc = acc_s[blk]
        o0 = acc[0:dv, 0:tq] / acc[dv:dv + 1, 0:tq]
        o1 = acc[0:dv, tq:] / acc[dv:dv + 1, tq:]
        a = o0 - lam * o1
        ms = jnp.mean(a * a, axis=0, keepdims=True)
        y = (a * lax.rsqrt(ms + RMS_EPS)).T * sg_ref[...]
        o_ref[pl.ds(pl.multiple_of(qblock * tq, tq), tq), :] = (y * (1.0 - lambda_init)).astype(o_ref.dtype)


def _attention(qt, k, vt, lq1, lk1, lq2, lk2, subln_g, lambda_init):
    B, _, S = qt.shape
    tq = ATTN_TILE
    n_q = S // tq
    assert S % tq == 0 and n_q % 2 == 0
    ones_rows = 16
    vec = lambda n: pl.BlockSpec((1, n), lambda b, h, p: (0, 0))
    return pl.pallas_call(
        functools.partial(_attn_kernel, lambda_init=lambda_init, n_q=n_q),
        grid=(B, N_DIFF_HEADS, n_q // 2),
        in_specs=[pl.BlockSpec((None, DIFF_V_DIM, tq), lambda b, h, p: (b, h, p)),
                  pl.BlockSpec((None, DIFF_V_DIM, tq), lambda b, h, p: (b, h, n_q - 1 - p)),
                  pl.BlockSpec((None, S, DIFF_V_DIM), lambda b, h, p: (b, 0, h)),
                  pl.BlockSpec((None, DIFF_V_DIM, S), lambda b, h, p: (b, h, 0)),
                  vec(DIFF_QK_DIM), vec(DIFF_QK_DIM), vec(DIFF_QK_DIM), vec(DIFF_QK_DIM),
                  vec(DIFF_V_DIM)],
        out_specs=pl.BlockSpec((None, S, DIFF_V_DIM), lambda b, h, p: (b, 0, h)),
        out_shape=jax.ShapeDtypeStruct((B, S, D_ATTN), BF16),
        scratch_shapes=[pltpu.VMEM((2, DIFF_V_DIM, 2 * tq), BF16),
                        pltpu.VMEM((n_q, DIFF_V_DIM + ones_rows, tq), BF16),
                        pltpu.VMEM((2, 1, 2 * tq), F32),
                        pltpu.VMEM((2, DIFF_V_DIM + ones_rows, 2 * tq), F32),
                        pltpu.VMEM((n_q + 1, tq, 2 * tq), F32)],
        compiler_params=_params(("arbitrary", "arbitrary", "arbitrary")),
        name="diff_attn",
    )(qt, qt, k, vt, lq1, lk1, lq2, lk2, subln_g)


def _pool_kernel(u_ref, w_ref, b_ref, sc_ref, o_ref):
    S = u_ref.shape[0]
    row = lax.broadcasted_iota(jnp.int32, (S, POOL_GROUP_DIM), 0)
    for g, w in enumerate(POOL_WINDOWS):
        cols = slice(g * POOL_GROUP_DIM, (g + 1) * POOL_GROUP_DIM)
        ug = u_ref[:, cols]
        s = ug
        span = 1
        while span < w:
            shifted = pltpu.roll(s, shift=span, axis=0)
            s = s + jnp.where(row >= span, shifted, 0.0)
            span *= 2
        cnt = jnp.minimum(row + 1, w).astype(F32)
        z = (s / cnt - ug).astype(BF16)
        y = jnp.dot(z, w_ref[g], preferred_element_type=F32) + b_ref[:, cols]
        o_ref[:, cols] = (y * sc_ref[:, cols]).astype(o_ref.dtype)


def _pool(u, w_pool, b_pool, pool_scale):
    B, S, _ = u.shape
    blk = pl.BlockSpec((None, S, D_POOL), lambda b: (b, 0, 0))
    return pl.pallas_call(
        _pool_kernel,
        grid=(B,),
        in_specs=[blk,
                  pl.BlockSpec((len(POOL_WINDOWS), POOL_GROUP_DIM, POOL_GROUP_DIM), lambda b: (0, 0, 0)),
                  pl.BlockSpec((1, D_POOL), lambda b: (0, 0)),
                  pl.BlockSpec((1, D_POOL), lambda b: (0, 0))],
        out_specs=blk,
        out_shape=jax.ShapeDtypeStruct((B, S, D_POOL), BF16),
        compiler_params=_params(("arbitrary",)),
        name="pool_mixer",
    )(u, w_pool, b_pool, pool_scale)


def _outproj_kernel(a_ref, p_ref, x_ref, w_ref, g_ref, rw_ref, rb_ref, x1_ref, h_ref, lg_ref):
    x1 = (x_ref[...]
          + jnp.dot(a_ref[...], w_ref[0:D_ATTN, :], preferred_element_type=F32)
          + jnp.dot(p_ref[...], w_ref[D_ATTN:, :], preferred_element_type=F32))
    x1_ref[...] = x1
    ms = jnp.mean(x1 * x1, axis=-1, keepdims=True)
    h = x1 * lax.rsqrt(ms + RMS_EPS) * g_ref[...]
    _store_token_tiles(h_ref, h)
    h_hi = h.astype(BF16)
    h_lo = (h - h_hi.astype(F32)).astype(BF16)
    both = jnp.dot(h_hi, rw_ref[...], preferred_element_type=F32)
    lg = (both[:, :LANES] + both[:, LANES:]
          + jnp.dot(h_lo, rw_ref[:, :LANES], preferred_element_type=F32) + rb_ref[...])
    lg_ref[...] = lg.T[:N_EXPERTS]


def _outproj(attn, pool, x2, w_out, g2, rw, rb):
    T = x2.shape[0]
    tm = TOKEN_TILE
    row = lambda i: (i, 0)
    fixed = lambda i: (0, 0)
    pad = ((0, 0), (0, LANES - N_EXPERTS))
    rw_hi = rw.astype(BF16)
    rw_lo = (rw - rw_hi.astype(F32)).astype(BF16)
    rw = jnp.concatenate([jnp.pad(rw_hi, pad), jnp.pad(rw_lo, pad)], axis=1)
    rb = jnp.pad(rb, pad)
    return pl.pallas_call(
        _outproj_kernel,
        grid=(T // tm,),
        in_specs=[pl.BlockSpec((tm, D_ATTN), row),
                  pl.BlockSpec((tm, D_POOL), row),
                  pl.BlockSpec((tm, D_MODEL), row),
                  pl.BlockSpec((D_MODEL, D_MODEL), fixed),
                  pl.BlockSpec((1, D_MODEL), fixed),
                  pl.BlockSpec((D_MODEL, 2 * LANES), fixed),
                  pl.BlockSpec((1, LANES), fixed)],
        out_specs=[pl.BlockSpec((tm, D_MODEL), row),
                   pl.BlockSpec((tm * TILE_ROWS, LANES), row),
                   pl.BlockSpec((None, N_EXPERTS, tm), lambda i: (i, 0, 0))],
        out_shape=[jax.ShapeDtypeStruct((T, D_MODEL), F32),
                   jax.ShapeDtypeStruct((T * TILE_ROWS, LANES), F32),
                   jax.ShapeDtypeStruct((T // tm, N_EXPERTS, tm), F32)],
        compiler_params=_params(("arbitrary",)),
        name="outproj_router",
    )(attn, pool, x2, w_out, g2, rw, rb)


def _route_kernel(lt_ref, pos_ref, gate_ref, cnt_ref, idx_s, rank_s):
    nt, E, W = lt_ref.shape
    e_iota = lax.broadcasted_iota(jnp.int32, (E, W), 0)
    before = (lax.broadcasted_iota(jnp.int32, (W, W), 0)
              < lax.broadcasted_iota(jnp.int32, (W, W), 1)).astype(BF16)
    ones = jnp.ones((W, LANES), BF16)
    widen = lambda a: jnp.concatenate([a] * (W // LANES), axis=1)

    def phase1(i, running):
        v = lt_ref[i]
        sel = jnp.zeros((E, W), F32)
        tops, hots = [], []
        for k in range(TOP_K):
            m = jnp.max(v, axis=0, keepdims=True)
            idx = jnp.min(jnp.where(v == m, e_iota, E), axis=0, keepdims=True)
            hot = e_iota == idx
            v = jnp.where(hot, -jnp.inf, v)
            sel = sel + hot.astype(F32)
            idx_s[i, k:k + 1, :] = idx
            tops.append(m)
            hots.append(hot)
        selb = sel.astype(BF16)
        rank = jnp.dot(selb, before, preferred_element_type=F32) + widen(running)
        for k in range(TOP_K):
            rank_s[i, k:k + 1, :] = jnp.sum(jnp.where(hots[k], rank, 0.0), axis=0, keepdims=True)
        ex = [jnp.exp(t - tops[0]) for t in tops]
        den = ex[0] + ex[1] + ex[2] + ex[3]
        for k in range(TOP_K):
            gate_ref[i, k:k + 1, :] = ex[k] / den
        return running + jnp.dot(selb, ones, preferred_element_type=F32)

    counts = lax.fori_loop(0, nt, phase1, jnp.zeros((E, LANES), F32))
    cnt_ref[...] = counts.astype(jnp.int32)

    row = lax.broadcasted_iota(jnp.int32, (E, LANES), 0)
    incl = counts
    span = 1
    while span < E:
        incl = incl + jnp.where(row >= span, pltpu.roll(incl, shift=span, axis=0), 0.0)
        span *= 2
    start = widen(incl - counts)

    def phase2(i, c):
        for k in range(TOP_K):
            hot = e_iota == idx_s[i, k:k + 1, :]
            base = jnp.sum(jnp.where(hot, start, 0.0), axis=0, keepdims=True)
            pos_ref[i, k:k + 1, :] = (base + rank_s[i, k:k + 1, :]).astype(jnp.int32)
        return c

    lax.fori_loop(0, nt, phase2, 0)


def _route(logits_t):
    nt, E, W = logits_t.shape
    whole = lambda shape: pl.BlockSpec(shape, lambda: (0,) * len(shape))
    return pl.pallas_call(
        _route_kernel,
        in_specs=[whole((nt, E, W))],
        out_specs=[whole((nt, TOP_K, W)), whole((nt, TOP_K, W)), whole((E, LANES))],
        out_shape=[jax.ShapeDtypeStruct((nt, TOP_K, W), jnp.int32),
                   jax.ShapeDtypeStruct((nt, TOP_K, W), F32),
                   jax.ShapeDtypeStruct((E, LANES), jnp.int32)],
        scratch_shapes=[pltpu.VMEM((nt, TOP_K, W), jnp.int32),
                        pltpu.VMEM((nt, TOP_K, W), F32)],
        compiler_params=pltpu.CompilerParams(vmem_limit_bytes=VMEM_LIMIT),
        name="route",
    )(logits_t)


def _work_items(counts):
    tm = MOE_TILE
    ends = jnp.cumsum(counts)
    starts = ends - counts
    first_blk = starts // tm
    last_blk = jnp.where(counts > 0, (ends - 1) // tm, first_blk - 1)
    n_items = last_blk - first_blk + 1
    item_end = jnp.cumsum(n_items)
    item_start = item_end - n_items
    return starts, ends, first_blk, item_start, item_end


def _item_table(counts, n_blocks):
    tm = MOE_TILE
    starts, ends, first_blk, item_start, item_end = _work_items(counts)
    n_slots = n_blocks + N_EXPERTS - 1
    j = jnp.arange(n_slots, dtype=jnp.int32)
    e = jnp.minimum(jnp.sum(item_end[None, :] <= j[:, None], axis=1), N_EXPERTS - 1).astype(jnp.int32)
    real = j < item_end[-1]
    blk = jnp.where(real, first_blk[e] + j - item_start[e], n_blocks - 1)
    lo = jnp.where(real, jnp.maximum(starts[e], blk * tm) - blk * tm, 0)
    hi = jnp.where(real, jnp.minimum(ends[e], (blk + 1) * tm) - blk * tm, 0)
    last_e = e[jnp.maximum(item_end[-1] - 1, 0)]
    e = jnp.where(real, e, last_e)
    ids = jnp.arange(N_EXPERTS, dtype=jnp.int32)
    used = counts > 0
    order = jnp.cumsum(used.astype(jnp.int32)) - 1
    later = jnp.where(jnp.logical_and(used[None, :], ids[None, :] > ids[:, None]), ids[None, :], N_EXPERTS)
    nxt_of = jnp.min(later, axis=1)
    nxt_of = jnp.where(nxt_of == N_EXPERTS, -1, nxt_of)
    as_i32 = lambda a: a.astype(jnp.int32)
    return as_i32(blk), as_i32(e), as_i32(lo), as_i32(hi), as_i32(order[e] % 2), as_i32(nxt_of[e])


def _scatter_kernel(pos_ref, src_ref, dst_ref, sem):
    tg = src_ref.shape[0]
    for r in range(tg):
        for k in range(TOP_K):
            pltpu.make_async_copy(src_ref.at[r], dst_ref.at[pos_ref[0, 0, k * tg + r]],
                                  sem).start(priority=(r * TOP_K + k) % 2)
    for k in range(TOP_K):
        pltpu.make_async_copy(src_ref, dst_ref.at[pl.ds(0, tg)], sem).wait()


def _scatter_rows(src, pos_tiles):
    T = src.shape[0]
    tg = GATHER_TILE
    return pl.pallas_call(
        _scatter_kernel,
        grid=(T // tg,),
        in_specs=[pl.BlockSpec((1, 1, tg * TOP_K), lambda i: (i, 0, 0), memory_space=pltpu.SMEM),
                  pl.BlockSpec((tg, TILE_ROWS, LANES), lambda i: (i, 0, 0))],
        out_specs=pl.BlockSpec(memory_space=pl.ANY),
        out_shape=jax.ShapeDtypeStruct((T * TOP_K, TILE_ROWS, LANES), src.dtype),
        scratch_shapes=[pltpu.SemaphoreType.DMA(())],
        compiler_params=_params(("arbitrary",)),
        name="row_scatter",
    )(pos_tiles, src)


def _moe_kernel(blk_ref, e_ref, lo_ref, hi_ref, par_ref, nxt_ref, xs_ref, wg_hbm, bg_ref, wu_hbm,
                bu_ref, wd_hbm, bd_ref, y_ref, wg_s, wu_s, wd_s, wbuf, wsem):
    j = pl.program_id(0)
    jp = jnp.maximum(j - 1, 0)
    tm = MOE_TILE
    srcs = (wg_hbm, wu_hbm, wd_hbm)

    def weight_copies(expert, buffer):
        return [pltpu.make_async_copy(src.at[expert], wbuf.at[buffer, m], wsem.at[buffer])
                for m, src in enumerate(srcs)]

    @pl.when(j == 0)
    def _():
        for cp in weight_copies(e_ref[0], par_ref[0]):
            cp.start()

    @pl.when(jnp.logical_or(j == 0, e_ref[j] != e_ref[jp]))
    def _():
        buffer = par_ref[j]
        for cp in weight_copies(e_ref[j], buffer):
            cp.wait()
        rows = 128
        for c in range(D_MODEL // rows):
            sl = slice(c * rows, (c + 1) * rows)
            wg_s[sl, :] = wbuf[buffer, 0, sl, :].astype(BF16)
            wu_s[sl, :] = wbuf[buffer, 1, sl, :].astype(BF16)
            wd_s[sl, :] = wbuf[buffer, 2, sl, :].astype(BF16)

        @pl.when(nxt_ref[j] >= 0)
        def _():
            for cp in weight_copies(nxt_ref[j], 1 - buffer):
                cp.start()

    lo = lo_ref[j]
    hi = hi_ref[j]

    first_visit = jnp.logical_or(j == 0, blk_ref[j] != blk_ref[jp])
    half = tm // 2

    def tile_rows(r0, n, c):
        return pl.ds(r0 * TILE_ROWS + c, n, stride=TILE_ROWS)

    def mlp(r0, n):
        x = jnp.concatenate([xs_ref[tile_rows(r0, n, c), :].astype(BF16) for c in range(TILE_ROWS)],
                            axis=1)
        y = None
        for f in range(D_FF // FF_CHUNK):
            cols = slice(f * FF_CHUNK, (f + 1) * FF_CHUNK)
            g = jnp.dot(x, wg_s[:, cols], preferred_element_type=F32) + bg_ref[:, cols]
            g = jnp.minimum(g, SWIGLU_LIMIT)
            u = jnp.dot(x, wu_s[:, cols], preferred_element_type=F32) + bu_ref[:, cols]
            u = jnp.clip(u, -SWIGLU_LIMIT, SWIGLU_LIMIT)
            act = (g * jax.nn.sigmoid(SWIGLU_ALPHA * g) * (u + 1.0)).astype(BF16)
            part = jnp.dot(act, wd_s[cols, :], preferred_element_type=F32)
            y = part if y is None else y + part
        return y + bd_ref[...]

    @pl.when(jnp.logical_and(lo == 0, hi == tm))
    def _():
        y = mlp(0, tm)
        for c in range(TILE_ROWS):
            y_ref[tile_rows(0, tm, c), :] = y[:, c * LANES:(c + 1) * LANES]

    def masked_rows(r0, n):
        y = mlp(r0, n)
        rows = r0 + lax.broadcasted_iota(jnp.int32, (n, LANES), 0)
        mine = jnp.logical_and(rows >= lo, rows < hi)

        @pl.when(first_visit)
        def _():
            for c in range(TILE_ROWS):
                y_ref[tile_rows(r0, n, c), :] = jnp.where(mine, y[:, c * LANES:(c + 1) * LANES], 0.0)

        @pl.when(jnp.logical_not(first_visit))
        def _():
            for c in range(TILE_ROWS):
                sl = tile_rows(r0, n, c)
                y_ref[sl, :] = jnp.where(mine, y[:, c * LANES:(c + 1) * LANES], y_ref[sl, :])

    partial = jnp.logical_and(lo < hi, jnp.logical_or(lo > 0, hi < tm))
    spans_both = jnp.logical_and(lo < half, hi > half)

    @pl.when(jnp.logical_and(partial, spans_both))
    def _():
        masked_rows(0, tm)

    for r0 in (0, half):
        inside = jnp.logical_and(lo >= r0, hi <= r0 + half)

        @pl.when(jnp.logical_and(partial, inside))
        def _():
            masked_rows(r0, half)

            @pl.when(first_visit)
            def _():
                other = half - r0
                y_ref[pl.ds(other * TILE_ROWS, half * TILE_ROWS), :] = jnp.zeros(
                    (half * TILE_ROWS, LANES), y_ref.dtype)


def _moe_experts(items, xs, wg, bg, wu, bu, wd, bd):
    P = xs.shape[0] // TILE_ROWS
    tm = MOE_TILE
    n_items = items[0].shape[0]
    assert D_FF == D_MODEL
    tiles = pl.BlockSpec((tm * TILE_ROWS, LANES), lambda j, blk, e, lo, hi, par, nxt: (blk[j], 0))
    wspec = pl.BlockSpec(memory_space=pl.ANY)
    bspec = pl.BlockSpec((None, 1, D_FF), lambda j, blk, e, lo, hi, par, nxt: (e[j], 0, 0))
    grid_spec = pltpu.PrefetchScalarGridSpec(
        num_scalar_prefetch=6,
        grid=(n_items,),
        in_specs=[tiles, wspec, bspec, wspec, bspec, wspec, bspec],
        out_specs=tiles,
        scratch_shapes=[pltpu.VMEM((D_MODEL, D_FF), BF16),
                        pltpu.VMEM((D_MODEL, D_FF), BF16),
                        pltpu.VMEM((D_FF, D_MODEL), BF16),
                        pltpu.VMEM((2, 3, D_MODEL, D_FF), F32),
                        pltpu.SemaphoreType.DMA((2,))],
    )
    return pl.pallas_call(
        _moe_kernel,
        grid_spec=grid_spec,
        out_shape=jax.ShapeDtypeStruct((P * TILE_ROWS, LANES), F32),
        compiler_params=_params(("arbitrary",)),
        name="moe_experts",
    )(*items, xs, wg, bg, wu, bu, wd, bd)


def _combine_kernel(pos_ref, x1_ref, gt_ref, g_ref, ys_ref, o_ref, buf, sem):
    i = pl.program_id(0)
    n_tiles = pl.num_programs(0) - 1
    tm = x1_ref.shape[0]
    n_rows = TOP_K * tm

    @pl.when(i < n_tiles)
    def _():
        slot = i % 2
        for n in range(n_rows):
            src = ys_ref.at[pl.ds(pl.multiple_of(pos_ref[0, 0, n] * TILE_ROWS, TILE_ROWS), TILE_ROWS), :]
            pltpu.make_async_copy(src, buf.at[slot, pl.ds(n * TILE_ROWS, TILE_ROWS), :],
                                  sem.at[slot]).start(priority=n % 2)

    @pl.when(i > 0)
    def _():
        slot = (i - 1) % 2
        pltpu.make_async_copy(ys_ref.at[pl.ds(0, n_rows * TILE_ROWS), :], buf.at[slot], sem.at[slot]).wait()
        x = x1_ref[...]
        gates = gt_ref[...]
        for k in range(TOP_K):
            x = x + gates[:, k:k + 1] * _load_token_tiles(
                buf.at[slot, pl.ds(k * tm * TILE_ROWS, tm * TILE_ROWS), :], tm)
        ms = jnp.mean(x * x, axis=-1, keepdims=True)
        o_ref[...] = x * lax.rsqrt(ms + RMS_EPS) * g_ref[...]


def _combine(x1, ys, pos_tiles, gates, g):
    T = x1.shape[0]
    tm = COMBINE_TILE
    n_tiles = T // tm
    done = lambda i: (jnp.maximum(i - 1, 0), 0)
    return pl.pallas_call(
        _combine_kernel,
        grid=(n_tiles + 1,),
        in_specs=[pl.BlockSpec((1, 1, TOP_K * tm), lambda i: (jnp.minimum(i, n_tiles - 1), 0, 0),
                               memory_space=pltpu.SMEM),
                  pl.BlockSpec((tm, D_MODEL), done),
                  pl.BlockSpec((tm, TOP_K), done),
                  pl.BlockSpec((1, D_MODEL), lambda i: (0, 0)),
                  pl.BlockSpec(memory_space=pl.ANY)],
        out_specs=pl.BlockSpec((tm, D_MODEL), done),
        out_shape=jax.ShapeDtypeStruct((T, D_MODEL), F32),
        scratch_shapes=[pltpu.VMEM((2, TOP_K * tm * TILE_ROWS, LANES), F32),
                        pltpu.SemaphoreType.DMA((2,))],
        compiler_params=_params(("arbitrary",)),
        name="gather_combine_norm",
    )(pos_tiles, x1, gates, g, ys)


def kernel(x, norm1_g, w_in, lambda_q1, lambda_k1, lambda_q2, lambda_k2, subln_g, w_pool, b_pool,
           pool_scale, w_out, norm2_g, router_w, router_b, w_gate, b_gate, w_up, b_up, w_down,
           b_down, final_g):
    B, S, D = x.shape
    T = B * S
    l = 0
    lambda_init = 0.8 - 0.6 * math.exp(-0.3 * l)
    x2 = x.reshape(T, D)

    w = w_in[l]
    wqv_t = jnp.concatenate([w[:, :D_ATTN], w[:, 2 * D_ATTN:3 * D_ATTN]], axis=1).T.astype(BF16)
    qt, k, vt, u = _inproj(x2, norm1_g[l][None, :], wqv_t, w[:, D_ATTN:2 * D_ATTN].astype(BF16),
                           w[:, 3 * D_ATTN:].astype(BF16), B, S)
    attn = _attention(qt, k.reshape(B, S, D_ATTN), vt,
                      lambda_q1[l][None, :], lambda_k1[l][None, :],
                      lambda_q2[l][None, :], lambda_k2[l][None, :],
                      subln_g[l][None, :], lambda_init)
    pool = _pool(u.reshape(B, S, D_POOL), w_pool[l].astype(BF16),
                 b_pool[l].reshape(1, D_POOL), pool_scale[l][None, :])
    x1, h2, logits_t = _outproj(attn.reshape(T, D_ATTN), pool.reshape(T, D_POOL), x2,
                                w_out[l].astype(BF16), norm2_g[l][None, :],
                                router_w[l], router_b[l][None, :])

    pos_t, gates_t, counts = _route(logits_t)
    nt, _, W = pos_t.shape

    def slot_tiles(tile):
        return (pos_t.reshape(nt, TOP_K, W // tile, tile).transpose(0, 2, 1, 3)
                .reshape(T // tile, 1, TOP_K * tile))

    gates = gates_t.transpose(0, 2, 1).reshape(T, TOP_K)
    assert (T * TOP_K) % MOE_TILE == 0
    items = _item_table(counts[:, 0], T * TOP_K // MOE_TILE)

    xs = _scatter_rows(h2.reshape(T, TILE_ROWS, LANES), slot_tiles(GATHER_TILE))
    ys = _moe_experts(items, xs.reshape(-1, LANES),
                      w_gate[l], b_gate[l][:, None, :], w_up[l], b_up[l][:, None, :],
                      w_down[l], b_down[l][:, None, :])
    out = _combine(x1, ys, slot_tiles(COMBINE_TILE), gates, final_g[None, :])
    return out.reshape(B, S, D)
```
